```python
import math, functools
import jax, jax.numpy as jnp
from jax import lax
import numpy as np

D_MODEL = 1024
BATCH = 2
SEQ = 16384
DEPTH = 2
DEC_BATCH = 16
DEC_SEQ = 16
PAST_LEN = 1024

CHUNK = 64
N_MIXERS = 2
N_LAYERS_A = (DEPTH + N_MIXERS - 1) // N_MIXERS
N_LAYERS_B = DEPTH // N_MIXERS
N_HEADS = 16
HEAD_DIM = D_MODEL // N_HEADS
D_FF = 2816
CONV_W = 3
Q_BLOCK = 128
K_BLOCK = 64
LN_EPS = 1e-5
DN_ALPHA = (2 * DEPTH) ** 0.25
DN_BETA = (8 * DEPTH) ** -0.25
FORGET_BIAS_INIT = 2.0

kernel_name = 'fox_stickbreaking_convffn_stream_step'


def _layer_norm(x, g, b):
    xf = x.astype(jnp.float32)
    mu = jnp.mean(xf, axis=-1, keepdims=True)
    var = jnp.mean(jnp.square(xf - mu), axis=-1, keepdims=True)
    y = (xf - mu) * lax.rsqrt(var + LN_EPS) * g.astype(jnp.float32) + b.astype(jnp.float32)
    return y.astype(x.dtype)


def _sweep_query_blocks(attend, q_arrays, k_arrays, q_pos, k_pos, past):
    t = q_pos.shape[0]
    outs = []
    for start in range(0, t, Q_BLOCK):
        stop = min(start + Q_BLOCK, t)
        k_end = past + stop
        outs.append(attend(*(a[:, start:stop] for a in q_arrays), q_pos[start:stop],
                           *(a[:, :k_end] for a in k_arrays), k_pos[:k_end]))
    return outs[0] if len(outs) == 1 else jnp.concatenate(outs, axis=1)


def _fox_attend(q, fq, q_pos, k, v, fk, k_pos):
    s = jnp.einsum('bqhd,bkhd->bhqk', q, k).astype(jnp.float32) * (HEAD_DIM ** -0.5)
    decay = jnp.transpose(fq, (0, 2, 1))[..., :, None] - jnp.transpose(fk, (0, 2, 1))[..., None, :]
    mask = k_pos[None, :] <= q_pos[:, None]
    s = jnp.where(mask, s + decay, -jnp.inf)
    p = jax.nn.softmax(s, axis=-1)
    return jnp.einsum('bhqk,bkhd->bqhd', p.astype(v.dtype), v)


def _strict_upper(n, dtype):
    r = jnp.arange(n)
    return (r[:, None] > r[None, :]).astype(dtype)


def _sb_attend(q, q_pos, k, v, k_pos):
    b, tq = q.shape[0], q.shape[1]
    tk = k.shape[1]
    pad = (-tk) % K_BLOCK
    if pad:
        k = jnp.concatenate([k, jnp.zeros((b, pad) + k.shape[2:], k.dtype)], axis=1)
        v = jnp.concatenate([v, jnp.zeros((b, pad) + v.shape[2:], v.dtype)], axis=1)
        k_pos = jnp.concatenate([k_pos, tk + jnp.arange(pad)])
    n_kb = (tk + pad) // K_BLOCK
    z = jnp.einsum('bqhd,bkhd->bhqk', q, k).astype(jnp.float32) * (HEAD_DIM ** -0.5)
    mask = k_pos[None, :] < q_pos[:, None]
    log_1m_beta = jnp.where(mask, jax.nn.log_sigmoid(-z), 0.0)
    l_blk = log_1m_beta.reshape(b, N_HEADS, tq, n_kb, K_BLOCK)
    within = jnp.einsum('bhqnj,js->bhqns', l_blk, _strict_upper(K_BLOCK, jnp.float32),
                        precision=lax.Precision.HIGHEST)
    later_blocks = jnp.einsum('bhqm,mn->bhqn', jnp.sum(l_blk, axis=-1), _strict_upper(n_kb, jnp.float32),
                              precision=lax.Precision.HIGHEST)
    later = (within + later_blocks[..., None]).reshape(z.shape)
    a = jnp.where(mask, jnp.exp(jax.nn.log_sigmoid(z) + later), 0.0)
    return jnp.einsum('bhqk,bkhd->bqhd', a.astype(v.dtype), v)


def _split_heads(proj, i, b, t):
    return proj[..., i * D_MODEL:(i + 1) * D_MODEL].reshape(b, t, N_HEADS, HEAD_DIM)


def _fox_mixer(x, past_k, past_v, past_logf, w_in, b_f, w_o):
    b, t, _ = x.shape
    p = past_k.shape[1]
    proj = x @ w_in
    q, k, v = (_split_heads(proj, i, b, t) for i in range(3))
    logf = jax.nn.log_sigmoid((proj[..., 3 * D_MODEL:] + b_f).astype(jnp.float32))
    k_all = jnp.concatenate([past_k.astype(k.dtype), k], axis=1)
    v_all = jnp.concatenate([past_v.astype(v.dtype), v], axis=1)
    f_cum = jnp.cumsum(jnp.concatenate([past_logf.astype(jnp.float32), logf], axis=1), axis=1)
    k_pos = jnp.arange(p + t)
    q_pos = p + jnp.arange(t)
    o = _sweep_query_blocks(_fox_attend, (q, f_cum[:, p:]), (k_all, v_all, f_cum), q_pos, k_pos, p)
    return o.reshape(b, t, D_MODEL) @ w_o, k, v, logf


def _sb_mixer(x, past_k, past_v, w_in, w_o):
    b, t, _ = x.shape
    p = past_k.shape[1]
    proj = x @ w_in
    q, k, v = (_split_heads(proj, i, b, t) for i in range(3))
    k_all = jnp.concatenate([past_k.astype(k.dtype), k], axis=1)
    v_all = jnp.concatenate([past_v.astype(v.dtype), v], axis=1)
    k_pos = jnp.arange(p + t)
    q_pos = p + jnp.arange(t)
    o = _sweep_query_blocks(_sb_attend, (q,), (k_all, v_all), q_pos, k_pos, p)
    return o.reshape(b, t, D_MODEL) @ w_o, k, v


def _conv_ffn(x, conv_prev, w_up, w_conv, b_conv, w_down):
    t = x.shape[1]
    u = x @ w_up
    ext = jnp.concatenate([conv_prev.astype(u.dtype), u], axis=1)
    c = b_conv
    for j in range(CONV_W):
        c = c + w_conv[j] * ext[:, j:j + t]
    a, g = jnp.split(c, 2, axis=-1)
    return (a * jax.nn.gelu(g)) @ w_down, ext[:, t:]


def setup_inputs(seed: int = 0) -> dict:
    key = jax.random.key(seed)
    ks = jax.random.split(key, 32)
    f32 = jnp.float32

    def nrm(k, shape, scale):
        return jax.random.normal(k, shape, f32) * scale

    s_in = D_MODEL ** -0.5
    x_prompt = nrm(ks[0], (BATCH, SEQ, D_MODEL), 1.0)
    x_sample = nrm(ks[1], (DEC_BATCH, DEC_SEQ, D_MODEL), 1.0)
    cache_a_k = nrm(ks[2], (N_LAYERS_A, DEC_BATCH, PAST_LEN, N_HEADS, HEAD_DIM), 1.0)
    cache_a_v = nrm(ks[3], (N_LAYERS_A, DEC_BATCH, PAST_LEN, N_HEADS, HEAD_DIM), DN_BETA)
    cache_a_logf = jax.nn.log_sigmoid(nrm(ks[4], (N_LAYERS_A, DEC_BATCH, PAST_LEN, N_HEADS), 1.0) + FORGET_BIAS_INIT)
    cache_b_k = nrm(ks[5], (N_LAYERS_B, DEC_BATCH, PAST_LEN, N_HEADS, HEAD_DIM), 1.0)
    cache_b_v = nrm(ks[6], (N_LAYERS_B, DEC_BATCH, PAST_LEN, N_HEADS, HEAD_DIM), DN_BETA)
    state_conv = nrm(ks[7], (DEPTH, DEC_BATCH, CONV_W - 1, 2 * D_FF), 0.5)
    w_a_in = jnp.concatenate([
        nrm(ks[8], (N_LAYERS_A, D_MODEL, 2 * D_MODEL), s_in),
        nrm(ks[9], (N_LAYERS_A, D_MODEL, D_MODEL), s_in * DN_BETA),
        nrm(ks[10], (N_LAYERS_A, D_MODEL, N_HEADS), s_in)], axis=-1)
    b_a_f = FORGET_BIAS_INIT + nrm(ks[11], (N_LAYERS_A, N_HEADS), 0.5)
    w_a_o = nrm(ks[12], (N_LAYERS_A, D_MODEL, D_MODEL), s_in * DN_BETA)
    w_b_in = jnp.concatenate([
        nrm(ks[13], (N_LAYERS_B, D_MODEL, 2 * D_MODEL), s_in),
        nrm(ks[14], (N_LAYERS_B, D_MODEL, D_MODEL), s_in * DN_BETA)], axis=-1)
    w_b_o = nrm(ks[15], (N_LAYERS_B, D_MODEL, D_MODEL), s_in * DN_BETA)
    ln1_g = 1.0 + nrm(ks[16], (DEPTH, D_MODEL), 0.05)
    ln1_b = nrm(ks[17], (DEPTH, D_MODEL), 0.01)
    w_up = nrm(ks[18], (DEPTH, D_MODEL, 2 * D_FF), s_in * DN_BETA)
    w_conv = nrm(ks[19], (DEPTH, CONV_W, 2 * D_FF), CONV_W ** -0.5)
    b_conv = nrm(ks[20], (DEPTH, 2 * D_FF), 0.01)
    w_down = nrm(ks[21], (DEPTH, D_FF, D_MODEL), D_FF ** -0.5 * DN_BETA)
    ln2_g = 1.0 + nrm(ks[22], (DEPTH, D_MODEL), 0.05)
    ln2_b = nrm(ks[23], (DEPTH, D_MODEL), 0.01)
    return {'x_prompt': x_prompt, 'x_sample': x_sample,
            'cache_a_k': cache_a_k, 'cache_a_v': cache_a_v, 'cache_a_logf': cache_a_logf,
            'cache_b_k': cache_b_k, 'cache_b_v': cache_b_v, 'state_conv': state_conv,
            'w_a_in': w_a_in, 'b_a_f': b_a_f, 'w_a_o': w_a_o,
            'w_b_in': w_b_in, 'w_b_o': w_b_o,
            'ln1_g': ln1_g, 'ln1_b': ln1_b,
            'w_up': w_up, 'w_conv': w_conv, 'b_conv': b_conv, 'w_down': w_down,
            'ln2_g': ln2_g, 'ln2_b': ln2_b}


def reference(x_prompt, x_sample, cache_a_k, cache_a_v, cache_a_logf, cache_b_k, cache_b_v, state_conv,
              w_a_in, b_a_f, w_a_o, w_b_in, w_b_o, ln1_g, ln1_b,
              w_up, w_conv, b_conv, w_down, ln2_g, ln2_b):
    bp = x_prompt.shape[0]
    empty_rows = jnp.zeros((bp, 0, N_HEADS, HEAD_DIM), x_prompt.dtype)
    empty_logf = jnp.zeros((bp, 0, N_HEADS), jnp.float32)
    zero_conv = jnp.zeros((bp, CONV_W - 1, 2 * D_FF), x_prompt.dtype)
    xp, xs = x_prompt, x_sample
    p_a_k, p_a_v, p_a_f, p_b_k, p_b_v, p_conv = [], [], [], [], [], []
    s_a_k, s_a_v, s_a_f, s_b_k, s_b_v, s_conv = [], [], [], [], [], []
    for i in range(DEPTH):
        j = i // N_MIXERS
        if i % N_MIXERS == 0:
            mp, kp, vp, lfp = _fox_mixer(xp, empty_rows, empty_rows, empty_logf, w_a_in[j], b_a_f[j], w_a_o[j])
            ms, ks_, vs, lfs = _fox_mixer(xs, cache_a_k[j], cache_a_v[j], cache_a_logf[j], w_a_in[j], b_a_f[j], w_a_o[j])
            p_a_k.append(kp); p_a_v.append(vp); p_a_f.append(lfp)
            s_a_k.append(ks_); s_a_v.append(vs); s_a_f.append(lfs)
        else:
            mp, kp, vp = _sb_mixer(xp, empty_rows, empty_rows, w_b_in[j], w_b_o[j])
            ms, ks_, vs = _sb_mixer(xs, cache_b_k[j], cache_b_v[j], w_b_in[j], w_b_o[j])
            p_b_k.append(kp); p_b_v.append(vp)
            s_b_k.append(ks_); s_b_v.append(vs)
        hp = _layer_norm(DN_ALPHA * xp + mp, ln1_g[i], ln1_b[i])
        hs = _layer_norm(DN_ALPHA * xs + ms, ln1_g[i], ln1_b[i])
        fp_out, cp = _conv_ffn(hp, zero_conv, w_up[i], w_conv[i], b_conv[i], w_down[i])
        fs_out, cs = _conv_ffn(hs, state_conv[i], w_up[i], w_conv[i], b_conv[i], w_down[i])
        p_conv.append(cp); s_conv.append(cs)
        xp = _layer_norm(DN_ALPHA * hp + fp_out, ln2_g[i], ln2_b[i])
        xs = _layer_norm(DN_ALPHA * hs + fs_out, ln2_g[i], ln2_b[i])
    return (xp, xs,
            jnp.stack(p_a_k), jnp.stack(p_a_v), jnp.stack(p_a_f),
            jnp.stack(p_b_k), jnp.stack(p_b_v), jnp.stack(p_conv),
            jnp.stack(s_a_k), jnp.stack(s_a_v), jnp.stack(s_a_f),
            jnp.stack(s_b_k), jnp.stack(s_b_v), jnp.stack(s_conv))
```

```python
import functools
import math

import numpy as np
import jax
import jax.numpy as jnp
from jax import lax
from jax.experimental import pallas as pl
from jax.experimental.pallas import tpu as pltpu

F32 = jnp.float32
BF16 = jnp.bfloat16
LOG2E = 1.4426950408889634
LN_EPS = 1e-5
LANES = 128
NEG_BIG = -1e30
ZERO_LOG2 = 160.0
VMEM_LIMIT = 56 * 1024 * 1024
GELU_C = 0.7978845608028654


def _cparams(*sem):
    return pltpu.CompilerParams(dimension_semantics=sem, vmem_limit_bytes=VMEM_LIMIT)


def _resident(shape):
    nd = len(shape)
    return pl.BlockSpec(shape, lambda *_: (0,) * nd, pipeline_mode=pl.Buffered(1))


def _in_head(lane, h, hd):
    return lane < hd if h == 0 else lane >= hd


def _log_sigmoid(x):
    return jnp.minimum(x, 0.0) - jnp.log(1.0 + jnp.exp(-jnp.abs(x)))


def _layer_norm(y, g, b):
    mu = jnp.mean(y, axis=-1, keepdims=True)
    yc = y - mu
    var = jnp.mean(yc * yc, axis=-1, keepdims=True)
    return yc * lax.rsqrt(var + LN_EPS) * g + b


def _inproj_kernel(x_ref, w_ref, *rest, has_f, qscale, n_heads):
    if has_f:
        wf_ref, bf_ref, q_ref, kb_ref, vb_ref, k_ref, v_ref, lf_ref = rest
    else:
        q_ref, kb_ref, vb_ref, k_ref, v_ref = rest
    xb = x_ref[...].astype(BF16)
    q = jnp.dot(xb, w_ref[0], preferred_element_type=F32)
    q_ref[...] = (q * qscale).astype(BF16)
    k = jnp.dot(xb, w_ref[1], preferred_element_type=F32)
    k_ref[...] = k
    kb_ref[...] = k.astype(BF16)
    v = jnp.dot(xb, w_ref[2], preferred_element_type=F32)
    v_ref[...] = v
    vb_ref[...] = v.astype(BF16)
    if has_f:
        z = jnp.dot(xb, wf_ref[...], preferred_element_type=F32) + bf_ref[...]
        lf_ref[...] = _log_sigmoid(z)[:, :n_heads]


def _inproj(x, w3, wf, bf, *, tm, qscale, n_heads):
    r, d = x.shape
    has_f = wf is not None
    row = lambda i: (i, 0)
    in_specs = [pl.BlockSpec((tm, d), row), _resident(w3.shape)]
    args = [x, w3]
    out_shape = [jax.ShapeDtypeStruct((r, d), BF16)] * 3 + [jax.ShapeDtypeStruct((r, d), F32)] * 2
    out_specs = [pl.BlockSpec((tm, d), row)] * 5
    if has_f:
        in_specs += [_resident(wf.shape), _resident(bf.shape)]
        args += [wf, bf]
        out_shape.append(jax.ShapeDtypeStruct((r, n_heads), F32))
        out_specs.append(pl.BlockSpec((tm, n_heads), row))
    return pl.pallas_call(
        functools.partial(_inproj_kernel, has_f=has_f, qscale=qscale, n_heads=n_heads),
        grid=(r // tm,), in_specs=in_specs, out_specs=out_specs, out_shape=out_shape,
        compiler_params=_cparams("parallel"), name="inproj_f" if has_f else "inproj",
    )(*args)


def _cumsum_kernel(x_ref, u_ref, o_ref, carry_ref, *, tc):
    @pl.when(pl.program_id(0) == 0)
    def _():
        carry_ref[...] = jnp.zeros_like(carry_ref)

    x = x_ref[...]
    r = x.shape[0]
    hi = x.astype(BF16)
    r1 = x - hi.astype(F32)
    mid = r1.astype(BF16)
    lo = (r1 - mid.astype(F32)).astype(BF16)
    y3 = jnp.dot(jnp.concatenate([hi, mid, lo], axis=0), u_ref[...], preferred_element_type=F32)
    y = y3[:r] + y3[r:2 * r] + y3[2 * r:]
    c = carry_ref[...]
    o_ref[...] = (y[:, :tc] + jnp.tile(c, (1, tc // LANES))) * LOG2E
    carry_ref[...] = c + y[:, tc:]


def _prefix_matrix(tc):
    j = np.arange(tc)[:, None]
    s = np.arange(tc + LANES)[None, :]
    return jnp.asarray(np.where((s >= tc) | (j <= s), 1.0, 0.0), dtype=BF16)


def _cumsum_log2(x):
    r, t = x.shape
    tc = 512 if t % 512 == 0 else t
    return pl.pallas_call(
        functools.partial(_cumsum_kernel, tc=tc),
        grid=(t // tc,),
        in_specs=[pl.BlockSpec((r, tc), lambda i: (0, i)), _resident((tc, tc + LANES))],
        out_specs=pl.BlockSpec((r, tc), lambda i: (0, i)),
        out_shape=jax.ShapeDtypeStruct((r, t), F32),
        scratch_shapes=[pltpu.VMEM((r, LANES), F32)],
        compiler_params=_cparams("arbitrary"), name="cumsum",
    )(x, _prefix_matrix(tc))


def _fox_kernel(q_ref, k_ref, v_ref, f_ref, o_ref, kn_scr, fm_scr, *, tq, tk, nkb, q_off, hd):
    qi = pl.program_id(2)

    @pl.when(qi == 0)
    def _stats():
        kn_scr[...] = jnp.zeros_like(kn_scr)
        fm_scr[...] = jnp.zeros_like(fm_scr)
        lane1 = lax.broadcasted_iota(jnp.int32, (1, LANES), 1)

        def body(j, _):
            start = pl.multiple_of(j * tk, tk)
            kb = k_ref[0, pl.ds(start, tk), :].astype(F32)
            km = jnp.max(kb * kb, axis=0, keepdims=True)
            for h in range(2):
                n2 = jnp.sum(jnp.where(_in_head(lane1, h, hd), km, 0.0), axis=1, keepdims=True)
                kn_scr[h, pl.ds(j, 1), :] = jnp.broadcast_to(jnp.sqrt(n2), (1, LANES))
                fmin = jnp.min(f_ref[0, 0, h, pl.ds(j, 1), :], axis=1, keepdims=True)
                fm_scr[h, pl.ds(j, 1), :] = jnp.broadcast_to(fmin, (1, LANES))
            return 0

        lax.fori_loop(0, nkb, body, 0)

    q = q_ref[0]
    lane = lax.broadcasted_iota(jnp.int32, (tq, LANES), 1)
    row = lax.broadcasted_iota(jnp.int32, (tq, tk), 0)
    col = lax.broadcasted_iota(jnp.int32, (tq, tk), 1)
    q_start = q_off + qi * tq
    jd = q_start // tk
    dmask = (jd * tk + col) <= (q_start + row)

    qh, qn, cc = [], [], []
    for h in range(2):
        qm = jnp.where(_in_head(lane, h, hd), q, jnp.zeros_like(q))
        qf = qm.astype(F32)
        qmax = jnp.max(qf * qf, axis=0, keepdims=True)
        qn.append(jnp.sqrt(jnp.sum(qmax, axis=1, keepdims=True)))
        cc.append(f_ref[0, 0, h, pl.ds(jd, 1), :][:, 0:1])
        qh.append(qm)

    def step(j, carry, masked):
        start = pl.multiple_of(j * tk, tk)
        kblk = k_ref[0, pl.ds(start, tk), :]
        vblk = v_ref[0, pl.ds(start, tk), :]
        out = []
        for h in range(2):
            m, l, acc = carry[3 * h:3 * h + 3]
            s = lax.dot_general(qh[h], kblk, (((1,), (1,)), ((), ())), preferred_element_type=F32)
            s = s + (cc[h] - f_ref[0, 0, h, pl.ds(j, 1), :])
            if masked:
                s = jnp.where(dmask, s, NEG_BIG)
            m_new = jnp.maximum(m, jnp.max(s, axis=1, keepdims=True))
            p = jnp.exp2(s - m_new)
            a = jnp.exp2(m - m_new)
            l = a * l + jnp.sum(p, axis=1, keepdims=True)
            acc = a * acc + jnp.dot(p.astype(BF16), vblk, preferred_element_type=F32)
            out += [m_new, l, acc]
        return tuple(out)

    init = []
    for h in range(2):
        init += [jnp.full((tq, 1), NEG_BIG, F32), jnp.zeros((tq, 1), F32), jnp.zeros((tq, LANES), F32)]
    carry = step(jd, tuple(init), True)

    jrow = lax.broadcasted_iota(jnp.int32, kn_scr.shape[1:], 0)
    jd_f = jd.astype(F32)
    lo = jd_f
    for h in range(2):
        m_min = jnp.min(carry[3 * h], axis=0, keepdims=True)
        bound = qn[h] * kn_scr[h] + cc[h] - fm_scr[h]
        need = (bound > m_min - ZERO_LOG2) & (jrow < jd)
        lo = jnp.minimum(lo, jnp.min(jnp.where(need, jrow.astype(F32), jd_f)))
    carry = lax.fori_loop(lo.astype(jnp.int32), jd, lambda j, c: step(j, c, False), carry)

    o = jnp.where(_in_head(lane, 0, hd), carry[2] / carry[1], carry[5] / carry[4])
    o_ref[0] = o.astype(BF16)


def _fox_attention(q, k, v, f2, *, tq, tk, q_off, hd):
    b, t_q, d = q.shape
    t_k = k.shape[1]
    nkb = t_k // tk
    nkb_pad = -(-nkb // 8) * 8
    return pl.pallas_call(
        functools.partial(_fox_kernel, tq=tq, tk=tk, nkb=nkb, q_off=q_off, hd=hd),
        grid=(b, d // LANES, t_q // tq),
        in_specs=[pl.BlockSpec((1, tq, LANES), lambda bi, hp, qi: (bi, qi, hp)),
                  pl.BlockSpec((1, t_k, LANES), lambda bi, hp, qi: (bi, 0, hp)),
                  pl.BlockSpec((1, t_k, LANES), lambda bi, hp, qi: (bi, 0, hp)),
                  pl.BlockSpec((1, 1, 2, nkb, tk), lambda bi, hp, qi: (bi, hp, 0, 0, 0))],
        out_specs=pl.BlockSpec((1, tq, LANES), lambda bi, hp, qi: (bi, qi, hp)),
        out_shape=jax.ShapeDtypeStruct((b, t_q, d), BF16),
        scratch_shapes=[pltpu.VMEM((2, nkb_pad, LANES), F32), pltpu.VMEM((2, nkb_pad, LANES), F32)],
        compiler_params=_cparams("parallel", "parallel", "arbitrary"), name="fox_attention",
    )(q, k, v, f2)


def _sb_kernel(q_ref, k_ref, v_ref, u_ref, o_ref, *, ng, tr, tb, q_off, hd):
    qi = pl.program_id(2)
    lane = lax.broadcasted_iota(jnp.int32, (tr, LANES), 1)
    row = lax.broadcasted_iota(jnp.int32, (tr, tb), 0)
    col = lax.broadcasted_iota(jnp.int32, (tr, tb), 1)
    u2 = u_ref[...]

    qh, jd, dmask = [], [], []
    for g in range(ng):
        q = q_ref[0, g * tr:(g + 1) * tr, :]
        q_start = q_off + (qi * ng + g) * tr
        jd.append(q_start // tb)
        dmask.append((jd[g] * tb + col) < (q_start + row))
        qh.append([jnp.where(_in_head(lane, h, hd), q, jnp.zeros_like(q)) for h in range(2)])

    def step(d, state, masked):
        out = []
        cont = jnp.int32(0)
        for g in range(ng):
            j = jd[g] - d
            live = j >= 0
            start = pl.multiple_of(jnp.maximum(j, 0) * tb, tb)
            kblk = k_ref[0, pl.ds(start, tb), :]
            vblk = v_ref[0, pl.ds(start, tb), :]
            dead_pen = jnp.where(live, 0.0, -NEG_BIG)
            for h in range(2):
                csum, acc = state[2 * (2 * g + h):2 * (2 * g + h) + 2]
                z = lax.dot_general(qh[g][h], kblk, (((1,), (1,)), ((), ())), preferred_element_type=F32)
                sp = jnp.maximum(z, 0.0) + jnp.log2(1.0 + jnp.exp2(-jnp.abs(z)))
                spm = jnp.where(dmask[g], sp, 0.0) if masked else sp
                hi = spm.astype(BF16)
                lo = (spm - hi.astype(F32)).astype(BF16)
                w = jnp.dot(jnp.concatenate([hi, lo], axis=1), u2, preferred_element_type=F32)
                a = jnp.exp2(z - sp - w[:, :tb] - (csum + dead_pen))
                if masked:
                    a = jnp.where(dmask[g], a, 0.0)
                acc = acc + jnp.dot(a.astype(BF16), vblk, preferred_element_type=F32)
                csum = csum + w[:, tb:]
                more = (jnp.min(csum) < ZERO_LOG2) & (j >= 1)
                cont = jnp.maximum(cont, more.astype(jnp.int32))
                out += [csum, acc]
        return d + 1, cont, tuple(out)

    init = tuple(jnp.zeros((tr, LANES), F32) for _ in range(4 * ng))
    d, cont, state = step(jnp.int32(0), init, True)
    _, _, state = lax.while_loop(lambda c: c[1] > 0, lambda c: step(c[0], c[2], False), (d, cont, state))

    for g in range(ng):
        o = jnp.where(_in_head(lane, 0, hd), state[4 * g + 1], state[4 * g + 3])
        o_ref[0, g * tr:(g + 1) * tr, :] = o.astype(BF16)


def _suffix_matrix(tb):
    j = (np.arange(2 * tb) % tb)[:, None]
    s = np.arange(2 * tb)[None, :]
    return jnp.asarray(np.where((s >= tb) | (j > s), 1.0, 0.0), dtype=BF16)


def _sb_attention(q, k, v, *, ng, tr, tb, q_off, hd):
    b, t_q, d = q.shape
    t_k = k.shape[1]
    tq = ng * tr
    return pl.pallas_call(
        functools.partial(_sb_kernel, ng=ng, tr=tr, tb=tb, q_off=q_off, hd=hd),
        grid=(b, d // LANES, t_q // tq),
        in_specs=[pl.BlockSpec((1, tq, LANES), lambda bi, hp, qi: (bi, qi, hp)),
                  pl.BlockSpec((1, t_k, LANES), lambda bi, hp, qi: (bi, 0, hp)),
                  pl.BlockSpec((1, t_k, LANES), lambda bi, hp, qi: (bi, 0, hp)),
                  _resident((2 * tb, 2 * tb))],
        out_specs=pl.BlockSpec((1, tq, LANES), lambda bi, hp, qi: (bi, qi, hp)),
        out_shape=jax.ShapeDtypeStruct((b, t_q, d), BF16),
        compiler_params=_cparams("parallel", "parallel", "arbitrary"), name="sb_attention",
    )(q, k, v, _suffix_matrix(tb))


def _outproj_kernel(o_ref, x_ref, w_ref, g_ref, b_ref, h_ref, *, alpha):
    y = alpha * x_ref[...] + jnp.dot(o_ref[...], w_ref[...], preferred_element_type=F32)
    h_ref[...] = _layer_norm(y, g_ref[...], b_ref[...])


def _outproj_ln(o, x, w, g, b, *, tm, alpha):
    r, d = x.shape
    row = lambda i: (i, 0)
    return pl.pallas_call(
        functools.partial(_outproj_kernel, alpha=alpha),
        grid=(r // tm,),
        in_specs=[pl.BlockSpec((tm, d), row), pl.BlockSpec((tm, d), row),
                  _resident(w.shape), _resident(g.shape), _resident(b.shape)],
        out_specs=pl.BlockSpec((tm, d), row),
        out_shape=jax.ShapeDtypeStruct((r, d), F32),
        compiler_params=_cparams("parallel"), name="outproj_ln",
    )(o, x, w, g, b)


def _ffn_kernel(h_ref, cp_ref, wu_ref, wc_ref, bc_ref, wd_ref, g_ref, b_ref, o_ref, cs_ref,
                *, ns, tl, nch, alpha):
    @pl.when(pl.program_id(1) == 0)
    def _():
        cs_ref[...] = cp_ref[...]

    h = h_ref[...]
    hb = h.astype(BF16)
    rows = ns * tl
    cw = wu_ref.shape[2]
    tmod = lax.broadcasted_iota(jnp.int32, (rows, cw), 0) & (tl - 1)

    def conv_half(idx):
        u = jnp.dot(hb, wu_ref[idx], preferred_element_type=F32)

        def prev_rows(r):
            return jnp.concatenate(
                [jnp.broadcast_to(cs_ref[s, idx, r:r + 1, :], (tl, cw)) for s in range(ns)], axis=0)

        p0, p1 = prev_rows(0), prev_rows(1)
        u1 = jnp.where(tmod == 0, p1, pltpu.roll(u, 1, 0))
        u2 = jnp.where(tmod == 0, p0, jnp.where(tmod == 1, p1, pltpu.roll(u, 2, 0)))
        for s in range(ns):
            cs_ref[s, idx] = u[(s + 1) * tl - 2:(s + 1) * tl, :]
        wc = wc_ref[idx]
        return bc_ref[idx] + wc[0:1] * u2 + wc[1:2] * u1 + wc[2:3] * u

    acc = jnp.zeros(h.shape, F32)
    for c in range(nch):
        a = conv_half(c)
        gt = conv_half(nch + c)
        gelu = 0.5 * gt * (1.0 + jnp.tanh(GELU_C * (gt + 0.044715 * (gt * gt * gt))))
        acc = acc + jnp.dot((a * gelu).astype(BF16), wd_ref[c], preferred_element_type=F32)
    o_ref[...] = _layer_norm(alpha * h + acc, g_ref[...], b_ref[...])


def _conv_ffn_ln(h, conv_prev, wu, wc, bc, wd, g, b, *, ns, tl, alpha):
    r, d = h.shape
    n_streams = conv_prev.shape[0]
    nch = wd.shape[0]
    nb = n_streams // ns
    nt = r // (n_streams * tl)
    assert ns == 1 or nt == 1
    assert tl & (tl - 1) == 0
    st_block = (ns,) + conv_prev.shape[1:]
    return pl.pallas_call(
        functools.partial(_ffn_kernel, ns=ns, tl=tl, nch=nch, alpha=alpha),
        grid=(nb, nt),
        in_specs=[pl.BlockSpec((ns * tl, d), lambda bi, ti: (bi * nt + ti, 0)),
                  pl.BlockSpec(st_block, lambda bi, ti: (bi, 0, 0, 0)),
                  _resident(wu.shape), _resident(wc.shape), _resident(bc.shape), _resident(wd.shape),
                  _resident(g.shape), _resident(b.shape)],
        out_specs=[pl.BlockSpec((ns * tl, d), lambda bi, ti: (bi * nt + ti, 0)),
                   pl.BlockSpec(st_block, lambda bi, ti: (bi, 0, 0, 0))],
        out_shape=[jax.ShapeDtypeStruct((r, d), F32), jax.ShapeDtypeStruct(conv_prev.shape, F32)],
        compiler_params=_cparams("parallel", "arbitrary"), name="conv_ffn_ln",
    )(h, conv_prev, wu, wc, bc, wd, g, b)


def _token_tile(r):
    for tm in (512, 256, 128, 64, 32, 16, 8):
        if r % tm == 0:
            return tm
    raise ValueError(f"row count {r} is not a multiple of 8")


def _pad_time(x, t_pad):
    return jnp.pad(x, ((0, 0), (0, t_pad - x.shape[1])) + ((0, 0),) * (x.ndim - 2))


def kernel(x_prompt, x_sample, cache_a_k, cache_a_v, cache_a_logf, cache_b_k, cache_b_v, state_conv,
           w_a_in, b_a_f, w_a_o, w_b_in, w_b_o, ln1_g, ln1_b, w_up, w_conv, b_conv, w_down, ln2_g, ln2_b):
    bp, tp, d = x_prompt.shape
    bs, ts, _ = x_sample.shape
    depth = ln1_g.shape[0]
    n_heads, hd = cache_a_k.shape[3], cache_a_k.shape[4]
    past = cache_a_k.shape[2]
    d_ff = w_down.shape[1]
    conv_w = w_conv.shape[1]
    assert 2 * hd == LANES and conv_w == 3 and d % LANES == 0
    alpha = (2.0 * depth) ** 0.25
    qscale = LOG2E * hd ** -0.5
    cw = 256
    assert d_ff % cw == 0
    nch = d_ff // cw
    tk_s = LANES
    tks_pad = -(-(past + ts) // tk_s) * tk_s
    fox_tile = 256 if tp % 256 == 0 else tp
    sb_rows = 128 if tp % 128 == 0 else tp
    sb_groups = 2 if tp % (2 * sb_rows) == 0 else 1

    xp = x_prompt.reshape(bp * tp, d)
    xs = x_sample.reshape(bs * ts, d)
    tm_p, tm_s = _token_tile(bp * tp), _token_tile(bs * ts)
    tl_p = _token_tile(tp)
    zero_conv = jnp.zeros((bp, 2 * nch, 2, cw), F32)

    def conv_state_in(s):
        return s.reshape(s.shape[0], 2, 2 * nch, cw).transpose(0, 2, 1, 3)

    def conv_state_out(s):
        return s.transpose(0, 2, 1, 3).reshape(s.shape[0], 2, 2 * d_ff)

    def heads(y, b, t):
        return y.reshape(b, t, n_heads, hd)

    outs = {n: [] for n in ("p_a_k", "p_a_v", "p_a_f", "p_b_k", "p_b_v", "p_conv",
                            "s_a_k", "s_a_v", "s_a_f", "s_b_k", "s_b_v", "s_conv")}
    for i in range(depth):
        j = i // 2
        fox = i % 2 == 0
        w_in = w_a_in[j] if fox else w_b_in[j]
        w3 = w_in[:, :3 * d].reshape(d, 3, d).transpose(1, 0, 2).astype(BF16)
        wf = bf = None
        if fox:
            wf = jnp.pad(w_in[:, 3 * d:], ((0, 0), (0, LANES - n_heads))).astype(BF16)
            bf = jnp.pad(b_a_f[j], (0, LANES - n_heads)).reshape(1, LANES)
        w_o = (w_a_o[j] if fox else w_b_o[j]).astype(BF16)
        g1, b1 = ln1_g[i].reshape(1, d), ln1_b[i].reshape(1, d)
        g2, b2 = ln2_g[i].reshape(1, d), ln2_b[i].reshape(1, d)
        wu = w_up[i].reshape(d, 2 * nch, cw).transpose(1, 0, 2).astype(BF16)
        wc = w_conv[i].reshape(conv_w, 2 * nch, cw).transpose(1, 0, 2)
        bc = b_conv[i].reshape(2 * nch, 1, cw)
        wd = w_down[i].reshape(nch, cw, d).astype(BF16)

        res = _inproj(xp, w3, wf, bf, tm=tm_p, qscale=qscale, n_heads=n_heads)
        q, kb, vb, k, v = res[:5]
        q, kb, vb = (a.reshape(bp, tp, d) for a in (q, kb, vb))
        if fox:
            lf = res[5].reshape(bp, tp, n_heads)
            f2 = _cumsum_log2(lf.transpose(0, 2, 1).reshape(bp * n_heads, tp))
            f2 = f2.reshape(bp, n_heads // 2, 2, tp // fox_tile, fox_tile)
            o = _fox_attention(q, kb, vb, f2, tq=fox_tile, tk=fox_tile, q_off=0, hd=hd)
            outs["p_a_k"].append(heads(k, bp, tp)); outs["p_a_v"].append(heads(v, bp, tp))
            outs["p_a_f"].append(lf)
        else:
            o = _sb_attention(q, kb, vb, ng=sb_groups, tr=sb_rows, tb=sb_rows, q_off=0, hd=hd)
            outs["p_b_k"].append(heads(k, bp, tp)); outs["p_b_v"].append(heads(v, bp, tp))
        hp = _outproj_ln(o.reshape(bp * tp, d), xp, w_o, g1, b1, tm=tm_p, alpha=alpha)
        xp, cs = _conv_ffn_ln(hp, zero_conv, wu, wc, bc, wd, g2, b2, ns=1, tl=tl_p, alpha=alpha)
        outs["p_conv"].append(conv_state_out(cs))

        res = _inproj(xs, w3, wf, bf, tm=tm_s, qscale=qscale, n_heads=n_heads)
        q, kb, vb, k, v = res[:5]
        q = q.reshape(bs, ts, d)
        ck, cv = (cache_a_k[j], cache_a_v[j]) if fox else (cache_b_k[j], cache_b_v[j])
        k_all = _pad_time(jnp.concatenate([ck.reshape(bs, past, d).astype(BF16), kb.reshape(bs, ts, d)], 1), tks_pad)
        v_all = _pad_time(jnp.concatenate([cv.reshape(bs, past, d).astype(BF16), vb.reshape(bs, ts, d)], 1), tks_pad)
        if fox:
            lf = res[5].reshape(bs, ts, n_heads)
            lf_all = _pad_time(jnp.concatenate([cache_a_logf[j].astype(F32), lf], 1), tks_pad)
            f2 = _cumsum_log2(lf_all.transpose(0, 2, 1).reshape(bs * n_heads, tks_pad))
            f2 = f2.reshape(bs, n_heads // 2, 2, tks_pad // tk_s, tk_s)
            o = _fox_attention(q, k_all, v_all, f2, tq=ts, tk=tk_s, q_off=past, hd=hd)
            outs["s_a_k"].append(heads(k, bs, ts)); outs["s_a_v"].append(heads(v, bs, ts))
            outs["s_a_f"].append(lf)
        else:
            o = _sb_attention(q, k_all, v_all, ng=1, tr=ts, tb=tk_s, q_off=past, hd=hd)
            outs["s_b_k"].append(heads(k, bs, ts)); outs["s_b_v"].append(heads(v, bs, ts))
        hs = _outproj_ln(o.reshape(bs * ts, d), xs, w_o, g1, b1, tm=tm_s, alpha=alpha)
        xs, cs = _conv_ffn_ln(hs, conv_state_in(state_conv[i]), wu, wc, bc, wd, g2, b2,
                              ns=bs, tl=ts, alpha=alpha)
        outs["s_conv"].append(conv_state_out(cs))

    st = {n: jnp.stack(v) for n, v in outs.items()}
    return (xp.reshape(bp, tp, d), xs.reshape(bs, ts, d),
            st["p_a_k"], st["p_a_v"], st["p_a_f"], st["p_b_k"], st["p_b_v"], st["p_conv"],
            st["s_a_k"], st["s_a_v"], st["s_a_f"], st["s_b_k"], st["s_b_v"], st["s_conv"])
```

```python
import functools

import numpy as np
import jax
import jax.numpy as jnp
from jax import lax
from jax.experimental import pallas as pl
from jax.experimental.pallas import tpu as pltpu

F32 = jnp.float32
BF16 = jnp.bfloat16
LOG2E = 1.4426950408889634
LN_EPS = 1e-5
LANES = 128
NEG_BIG = -1e30
ZERO_LOG2 = 150.0
NORM_SLACK = 1.0 + 2.0 ** -6
EXP_HEADROOM = 90.0
VMEM_LIMIT = 56 * 1024 * 1024
GELU_C = 0.7978845608028654


def _cparams(*sem):
    return pltpu.CompilerParams(dimension_semantics=sem, vmem_limit_bytes=VMEM_LIMIT)


def _resident(shape):
    nd = len(shape)
    return pl.BlockSpec(shape, lambda *_: (0,) * nd, pipeline_mode=pl.Buffered(1))


def _in_head(lane, h, hd):
    return lane < hd if h == 0 else lane >= hd


def _log_sigmoid(x):
    return jnp.minimum(x, 0.0) - jnp.log(1.0 + jnp.exp(-jnp.abs(x)))


def _layer_norm(y, g, b):
    mu = jnp.mean(y, axis=-1, keepdims=True)
    yc = y - mu
    var = jnp.mean(yc * yc, axis=-1, keepdims=True)
    return yc * lax.rsqrt(var + LN_EPS) * g + b


def _inproj_kernel(x_ref, w_ref, *rest, has_f, qscale, n_heads):
    if has_f:
        wf_ref, bf_ref, q_ref, kb_ref, vb_ref, k_ref, v_ref, lf_ref = rest
    else:
        q_ref, kb_ref, vb_ref, k_ref, v_ref = rest
    xb = x_ref[...].astype(BF16)
    q = jnp.dot(xb, w_ref[0], preferred_element_type=F32)
    q_ref[...] = (q * qscale).astype(BF16)
    k = jnp.dot(xb, w_ref[1], preferred_element_type=F32)
    k_ref[...] = k
    kb_ref[...] = k.astype(BF16)
    v = jnp.dot(xb, w_ref[2], preferred_element_type=F32)
    v_ref[...] = v
    vb_ref[...] = v.astype(BF16)
    if has_f:
        z = jnp.dot(xb, wf_ref[...], preferred_element_type=F32) + bf_ref[...]
        lf_ref[...] = _log_sigmoid(z)[:, :n_heads]


def _inproj(x, w3, wf, bf, *, tm, qscale, n_heads):
    r, d = x.shape
    has_f = wf is not None
    row = lambda i: (i, 0)
    in_specs = [pl.BlockSpec((tm, d), row), _resident(w3.shape)]
    args = [x, w3]
    out_shape = [jax.ShapeDtypeStruct((r, d), BF16)] * 3 + [jax.ShapeDtypeStruct((r, d), F32)] * 2
    out_specs = [pl.BlockSpec((tm, d), row)] * 5
    if has_f:
        in_specs += [_resident(wf.shape), _resident(bf.shape)]
        args += [wf, bf]
        out_shape.append(jax.ShapeDtypeStruct((r, n_heads), F32))
        out_specs.append(pl.BlockSpec((tm, n_heads), row))
    return pl.pallas_call(
        functools.partial(_inproj_kernel, has_f=has_f, qscale=qscale, n_heads=n_heads),
        grid=(r // tm,), in_specs=in_specs, out_specs=out_specs, out_shape=out_shape,
        compiler_params=_cparams("parallel"), name="inproj_f" if has_f else "inproj",
    )(*args)


def _cumsum_kernel(x_ref, u_ref, o_ref, carry_ref, *, tc):
    @pl.when(pl.program_id(0) == 0)
    def _():
        carry_ref[...] = jnp.zeros_like(carry_ref)

    x = x_ref[...]
    r = x.shape[0]
    hi = x.astype(BF16)
    r1 = x - hi.astype(F32)
    mid = r1.astype(BF16)
    lo = (r1 - mid.astype(F32)).astype(BF16)
    y3 = jnp.dot(jnp.concatenate([hi, mid, lo], axis=0), u_ref[...], preferred_element_type=F32)
    y = y3[:r] + y3[r:2 * r] + y3[2 * r:]
    c = carry_ref[...]
    o_ref[...] = (y[:, :tc] + jnp.tile(c, (1, tc // LANES))) * LOG2E
    carry_ref[...] = c + y[:, tc:]


def _prefix_matrix(tc):
    j = np.arange(tc)[:, None]
    s = np.arange(tc + LANES)[None, :]
    return jnp.asarray(np.where((s >= tc) | (j <= s), 1.0, 0.0), dtype=BF16)


def _cumsum_log2(x):
    r, t = x.shape
    tc = 512 if t % 512 == 0 else t
    return pl.pallas_call(
        functools.partial(_cumsum_kernel, tc=tc),
        grid=(t // tc,),
        in_specs=[pl.BlockSpec((r, tc), lambda i: (0, i)), _resident((tc, tc + LANES))],
        out_specs=pl.BlockSpec((r, tc), lambda i: (0, i)),
        out_shape=jax.ShapeDtypeStruct((r, t), F32),
        scratch_shapes=[pltpu.VMEM((r, LANES), F32)],
        compiler_params=_cparams("arbitrary"), name="cumsum",
    )(x, _prefix_matrix(tc))


def _head_sq_norms(x, hd):
    xf = x.astype(F32)
    lrow = lax.broadcasted_iota(jnp.int32, (LANES, LANES), 0)
    lcol = lax.broadcasted_iota(jnp.int32, (LANES, LANES), 1)
    ind = jnp.where((lrow >= hd).astype(jnp.int32) == lcol, 1.0, 0.0).astype(BF16)
    n2 = jnp.dot((xf * xf).astype(BF16), ind, preferred_element_type=F32)
    return jnp.max(n2, axis=0, keepdims=True) * NORM_SLACK


def _fox_kernel(q_ref, k_ref, v_ref, f_ref, o_ref, m_scr, acc_scr, kn_scr, fm_scr,
                *, tq, tk, nkb, q_off, hd):
    qi = pl.program_id(2)
    lane1 = lax.broadcasted_iota(jnp.int32, (1, LANES), 1)

    if nkb > 1:
        @pl.when(qi == 0)
        def _stats():
            kn_scr[...] = jnp.zeros_like(kn_scr)
            fm_scr[...] = jnp.zeros_like(fm_scr)

            def body(j, _):
                start = pl.multiple_of(j * tk, tk)
                kn_scr[pl.ds(j, 1), :] = jnp.sqrt(_head_sq_norms(k_ref[0, pl.ds(start, tk), :], hd))
                fmin = [jnp.min(f_ref[0, 0, h, pl.ds(j, 1), :], axis=1, keepdims=True) for h in range(2)]
                fm_scr[pl.ds(j, 1), :] = jnp.where(lane1 == 0, fmin[0], fmin[1])
                return 0

            lax.fori_loop(0, nkb, body, 0)

    q = q_ref[0]
    lane = lax.broadcasted_iota(jnp.int32, (tq, LANES), 1)
    row = lax.broadcasted_iota(jnp.int32, (tq, tk), 0)
    col = lax.broadcasted_iota(jnp.int32, (tq, tk), 1)
    q_start = q_off + qi * tq
    jd = q_start // tk
    dmask = (jd * tk + col) <= (q_start + row)
    qh = [jnp.where(_in_head(lane, h, hd), q, jnp.zeros_like(q)) for h in range(2)]
    cc = [f_ref[0, 0, h, pl.ds(jd, 1), :][:, 0:1] for h in range(2)]

    m_scr[...] = jnp.full(m_scr.shape, NEG_BIG, F32)
    acc_scr[...] = jnp.zeros_like(acc_scr)

    l_lane = (hd, 0)
    lane_k = lax.broadcasted_iota(jnp.int32, (tk, LANES), 1)
    fill = [jnp.where(lane_k == l_lane[h], 1.0, 0.0).astype(BF16) for h in range(2)]

    reps = tk // LANES

    def step(js, mode):
        blocks = []
        for j in js:
            start = pl.multiple_of(j * tk, tk)
            vblk = v_ref[0, pl.ds(start, tk), :]
            blocks.append((j, k_ref[0, pl.ds(start, tk), :],
                           [jnp.where(_in_head(lane_k, h, hd), vblk, fill[h]) for h in range(2)]))
        s = {}
        for b, (j, kblk, _) in enumerate(blocks):
            for h in range(2):
                x = lax.dot_general(qh[h], kblk, (((1,), (1,)), ((), ())), preferred_element_type=F32)
                x = x + (cc[h] - f_ref[0, 0, h, pl.ds(j, 1), :])
                s[b, h] = jnp.where(dmask, x, NEG_BIG) if mode == "diag" else x
        m_old = [m_scr[h] for h in range(2)]
        if mode == "fixed_max":
            p = {c: jnp.exp2(s[c] - jnp.tile(m_old[c[1]], (1, reps))).astype(BF16) for c in s}
            pv = {c: jnp.dot(p[c], blocks[c[0]][2][c[1]], preferred_element_type=F32) for c in s}
            for h in range(2):
                acc_scr[h] += sum(pv[b, h] for b in range(len(blocks)))
            return
        assert len(js) == 1
        m_new = [jnp.maximum(m_old[h], jnp.max(s[0, h], axis=1, keepdims=True)) for h in range(2)]
        p = [jnp.exp2(s[0, h] - jnp.tile(m_new[h], (1, reps))).astype(BF16) for h in range(2)]
        a = [jnp.exp2(m_old[h] - m_new[h]) for h in range(2)]
        for h in range(2):
            m_scr[h] = m_new[h]
        pv = [jnp.dot(p[h], blocks[0][2][h], preferred_element_type=F32) for h in range(2)]
        for h in range(2):
            acc_scr[h] = a[h] * acc_scr[h] + pv[h]

    step([jd], "diag")

    if nkb > 1:
        jrow = lax.broadcasted_iota(jnp.int32, (kn_scr.shape[0], 1), 0)
        jd_f = jd.astype(F32)
        qn = jnp.sqrt(_head_sq_norms(q, hd))
        lo = jd_f
        excess = jnp.float32(NEG_BIG)
        for h in range(2):
            m_min = jnp.min(m_scr[h], axis=0, keepdims=True)[:, 0:1]
            over = qn[:, h:h + 1] * kn_scr[:, h:h + 1] + cc[h] - fm_scr[:, h:h + 1] - m_min
            need = (over > -ZERO_LOG2) & (jrow < jd)
            lo = jnp.minimum(lo, jnp.min(jnp.where(need, jrow.astype(F32), jd_f)))
            excess = jnp.maximum(excess, jnp.max(jnp.where(need, over, NEG_BIG)))
        lo = lo.astype(jnp.int32)

        def fixed_sweep():
            def pair(i, _):
                step([lo + 2 * i, lo + 2 * i + 1], "fixed_max")
                return 0

            lax.fori_loop(0, (jd - lo) // 2, pair, 0)

            @pl.when((jd - lo) % 2 == 1)
            def _():
                step([jd - 1], "fixed_max")

        def running_sweep():
            def body(j, _):
                step([j], "running_max")
                return 0

            lax.fori_loop(lo, jd, body, 0)

        lax.cond(excess <= EXP_HEADROOM, fixed_sweep, running_sweep)

    acc = [acc_scr[h] for h in range(2)]
    den = [acc[h][:, l_lane[h]:l_lane[h] + 1] for h in range(2)]
    o = jnp.where(_in_head(lane, 0, hd), acc[0] / den[0], acc[1] / den[1])
    o_ref[0] = o.astype(BF16)


def _fox_attention(q, k, v, f2, *, tq, tk, q_off, hd):
    b, t_q, d = q.shape
    t_k = k.shape[1]
    nkb = t_k // tk
    nkb_pad = -(-nkb // 8) * 8
    assert tk % tq == 0 and q_off % tq == 0
    return pl.pallas_call(
        functools.partial(_fox_kernel, tq=tq, tk=tk, nkb=nkb, q_off=q_off, hd=hd),
        grid=(b, d // LANES, t_q // tq),
        in_specs=[pl.BlockSpec((1, tq, LANES), lambda bi, hp, qi: (bi, qi, hp)),
                  pl.BlockSpec((1, t_k, LANES), lambda bi, hp, qi: (bi, 0, hp)),
                  pl.BlockSpec((1, t_k, LANES), lambda bi, hp, qi: (bi, 0, hp)),
                  pl.BlockSpec((1, 1, 2, nkb, tk), lambda bi, hp, qi: (bi, hp, 0, 0, 0))],
        out_specs=pl.BlockSpec((1, tq, LANES), lambda bi, hp, qi: (bi, qi, hp)),
        out_shape=jax.ShapeDtypeStruct((b, t_q, d), BF16),
        scratch_shapes=[pltpu.VMEM((2, tq, LANES), F32), pltpu.VMEM((2, tq, LANES), F32),
                        pltpu.VMEM((nkb_pad, LANES), F32), pltpu.VMEM((nkb_pad, LANES), F32)],
        compiler_params=_cparams("parallel", "parallel", "arbitrary"), name="fox_attention",
    )(q, k, v, f2)


def _sb_kernel(q_ref, k_ref, v_ref, u_ref, o_ref, cs_scr, acc_scr, *, ng, tr, tb, q_off, hd):
    qi = pl.program_id(2)
    lane = lax.broadcasted_iota(jnp.int32, (tr, LANES), 1)
    row = lax.broadcasted_iota(jnp.int32, (tr, tb), 0)
    col = lax.broadcasted_iota(jnp.int32, (tr, tb), 1)
    u2 = u_ref[...]
    chains = [(g, h) for g in range(ng) for h in range(2)]

    qh, jd, dmask = {}, [], []
    for g in range(ng):
        q = q_ref[0, g * tr:(g + 1) * tr, :]
        q_start = q_off + (qi * ng + g) * tr
        jd.append(q_start // tb)
        dmask.append((jd[g] * tb + col) < (q_start + row))
        for h in range(2):
            qh[g, h] = jnp.where(_in_head(lane, h, hd), q, jnp.zeros_like(q))

    cs_scr[...] = jnp.zeros_like(cs_scr)
    acc_scr[...] = jnp.zeros_like(acc_scr)

    def step(d, masked):
        j = [jd[g] - d for g in range(ng)]
        start = [pl.multiple_of(jnp.maximum(j[g], 0) * tb, tb) for g in range(ng)]
        kblk = [k_ref[0, pl.ds(start[g], tb), :] for g in range(ng)]
        vblk = [v_ref[0, pl.ds(start[g], tb), :] for g in range(ng)]
        pen = [jnp.where(j[g] >= 0, 0.0, -NEG_BIG) for g in range(ng)]
        z = {c: lax.dot_general(qh[c], kblk[c[0]], (((1,), (1,)), ((), ())), preferred_element_type=F32)
             for c in chains}
        sp = {c: jnp.maximum(z[c], 0.0) + jnp.log2(1.0 + jnp.exp2(-jnp.abs(z[c]))) for c in chains}
        w = {}
        for c in chains:
            spm = jnp.where(dmask[c[0]], sp[c], 0.0) if masked else sp[c]
            hi = spm.astype(BF16)
            lo = (spm - hi.astype(F32)).astype(BF16)
            w[c] = jnp.dot(jnp.concatenate([hi, lo], axis=1), u2, preferred_element_type=F32)
        a = {}
        open_sum = None
        for i, c in enumerate(chains):
            cs_old = cs_scr[i]
            x = jnp.exp2(z[c] - sp[c] - w[c][:, :tb] - jnp.tile(cs_old + pen[c[0]], (1, tb // LANES)))
            a[c] = (jnp.where(dmask[c[0]], x, 0.0) if masked else x).astype(BF16)
            cs_new = cs_old + w[c][:, tb:tb + LANES]
            cs_scr[i] = cs_new
            cand = cs_new + jnp.where(j[c[0]] >= 1, 0.0, -NEG_BIG)
            open_sum = cand if open_sum is None else jnp.minimum(open_sum, cand)
        for i, c in enumerate(chains):
            acc_scr[i] += jnp.dot(a[c], vblk[c[0]], preferred_element_type=F32)
        return (jnp.min(open_sum) < ZERO_LOG2).astype(jnp.int32)

    cont = step(jnp.int32(0), True)
    lax.while_loop(lambda c: c[1] > 0, lambda c: (c[0] + 1, step(c[0], False)), (jnp.int32(1), cont))

    for g in range(ng):
        o = jnp.where(_in_head(lane, 0, hd), acc_scr[2 * g], acc_scr[2 * g + 1])
        o_ref[0, g * tr:(g + 1) * tr, :] = o.astype(BF16)


def _suffix_matrix(tb):
    j = (np.arange(2 * tb) % tb)[:, None]
    s = np.arange(tb + LANES)[None, :]
    return jnp.asarray(np.where((s >= tb) | (j > s), 1.0, 0.0), dtype=BF16)


def _sb_attention(q, k, v, *, ng, tr, tb, q_off, hd):
    b, t_q, d = q.shape
    t_k = k.shape[1]
    tq = ng * tr
    assert tb % tr == 0 and q_off % tr == 0
    return pl.pallas_call(
        functools.partial(_sb_kernel, ng=ng, tr=tr, tb=tb, q_off=q_off, hd=hd),
        grid=(b, d // LANES, t_q // tq),
        in_specs=[pl.BlockSpec((1, tq, LANES), lambda bi, hp, qi: (bi, qi, hp)),
                  pl.BlockSpec((1, t_k, LANES), lambda bi, hp, qi: (bi, 0, hp)),
                  pl.BlockSpec((1, t_k, LANES), lambda bi, hp, qi: (bi, 0, hp)),
                  _resident((2 * tb, tb + LANES))],
        out_specs=pl.BlockSpec((1, tq, LANES), lambda bi, hp, qi: (bi, qi, hp)),
        out_shape=jax.ShapeDtypeStruct((b, t_q, d), BF16),
        scratch_shapes=[pltpu.VMEM((2 * ng, tr, LANES), F32), pltpu.VMEM((2 * ng, tr, LANES), F32)],
        compiler_params=_cparams("parallel", "parallel", "arbitrary"), name="sb_attention",
    )(q, k, v, _suffix_matrix(tb))


def _outproj_kernel(o_ref, x_ref, w_ref, g_ref, b_ref, h_ref, *, alpha):
    y = alpha * x_ref[...] + jnp.dot(o_ref[...], w_ref[...], preferred_element_type=F32)
    h_ref[...] = _layer_norm(y, g_ref[...], b_ref[...])


def _outproj_ln(o, x, w, g, b, *, tm, alpha):
    r, d = x.shape
    row = lambda i: (i, 0)
    return pl.pallas_call(
        functools.partial(_outproj_kernel, alpha=alpha),
        grid=(r // tm,),
        in_specs=[pl.BlockSpec((tm, d), row), pl.BlockSpec((tm, d), row),
                  _resident(w.shape), _resident(g.shape), _resident(b.shape)],
        out_specs=pl.BlockSpec((tm, d), row),
        out_shape=jax.ShapeDtypeStruct((r, d), F32),
        compiler_params=_cparams("parallel"), name="outproj_ln",
    )(o, x, w, g, b)


def _ffn_kernel(h_ref, cp_ref, wu_ref, wc_ref, bc_ref, wd_ref, g_ref, b_ref, o_ref, cs_ref,
                sa_scr, sg_scr, *, ns, tl, nch, alpha):
    @pl.when(pl.program_id(1) == 0)
    def _():
        cs_ref[...] = cp_ref[...]

    h = h_ref[...]
    hb = h.astype(BF16)
    cw = wu_ref.shape[2]

    def up(idx):
        return jnp.dot(hb, wu_ref[idx], preferred_element_type=F32)

    def conv(idx, u, scr):
        for s in range(ns):
            scr[s, 6:8, :] = cs_ref[s, idx]
            scr[s, 8:8 + tl, :] = u[s * tl:(s + 1) * tl]
            cs_ref[s, idx] = u[(s + 1) * tl - 2:(s + 1) * tl, :]
        u1 = jnp.concatenate([scr[s, 7:7 + tl, :] for s in range(ns)], axis=0)
        u2 = jnp.concatenate([scr[s, 6:6 + tl, :] for s in range(ns)], axis=0)
        wc = wc_ref[idx]
        return bc_ref[idx] + wc[0:1] * u2 + wc[1:2] * u1 + wc[2:3] * u

    acc = jnp.zeros(h.shape, F32)
    ua, ug = up(0), up(nch)
    for c in range(nch):
        cur_a, cur_g = ua, ug
        if c + 1 < nch:
            ua, ug = up(c + 1), up(nch + c + 1)
        a = conv(c, cur_a, sa_scr)
        gt = conv(nch + c, cur_g, sg_scr)
        gelu = 0.5 * gt * (1.0 + jnp.tanh(GELU_C * (gt + 0.044715 * (gt * gt * gt))))
        acc = acc + jnp.dot((a * gelu).astype(BF16), wd_ref[c], preferred_element_type=F32)
    o_ref[...] = _layer_norm(alpha * h + acc, g_ref[...], b_ref[...])


def _conv_ffn_ln(h, conv_prev, wu, wc, bc, wd, g, b, *, ns, tl, alpha):
    r, d = h.shape
    n_streams = conv_prev.shape[0]
    nch = wd.shape[0]
    nb = n_streams // ns
    nt = r // (n_streams * tl)
    assert ns == 1 or nt == 1
    assert tl % 16 == 0
    st_block = (ns,) + conv_prev.shape[1:]
    return pl.pallas_call(
        functools.partial(_ffn_kernel, ns=ns, tl=tl, nch=nch, alpha=alpha),
        grid=(nb, nt),
        in_specs=[pl.BlockSpec((ns * tl, d), lambda bi, ti: (bi * nt + ti, 0)),
                  pl.BlockSpec(st_block, lambda bi, ti: (bi, 0, 0, 0)),
                  _resident(wu.shape), _resident(wc.shape), _resident(bc.shape), _resident(wd.shape),
                  _resident(g.shape), _resident(b.shape)],
        out_specs=[pl.BlockSpec((ns * tl, d), lambda bi, ti: (bi * nt + ti, 0)),
                   pl.BlockSpec(st_block, lambda bi, ti: (bi, 0, 0, 0))],
        out_shape=[jax.ShapeDtypeStruct((r, d), F32), jax.ShapeDtypeStruct(conv_prev.shape, F32)],
        scratch_shapes=[pltpu.VMEM((ns, tl + 8, conv_prev.shape[3]), F32)] * 2,
        compiler_params=_cparams("parallel", "arbitrary"), name="conv_ffn_ln",
    )(h, conv_prev, wu, wc, bc, wd, g, b)


def _token_tile(r):
    for tm in (512, 256, 128, 64, 32, 16, 8):
        if r % tm == 0:
            return tm
    raise ValueError(f"row count {r} is not a multiple of 8")


def _pad_time(x, t_pad):
    return jnp.pad(x, ((0, 0), (0, t_pad - x.shape[1])) + ((0, 0),) * (x.ndim - 2))


def kernel(x_prompt, x_sample, cache_a_k, cache_a_v, cache_a_logf, cache_b_k, cache_b_v, state_conv,
           w_a_in, b_a_f, w_a_o, w_b_in, w_b_o, ln1_g, ln1_b, w_up, w_conv, b_conv, w_down, ln2_g, ln2_b):
    bp, tp, d = x_prompt.shape
    bs, ts, _ = x_sample.shape
    depth = ln1_g.shape[0]
    n_heads, hd = cache_a_k.shape[3], cache_a_k.shape[4]
    past = cache_a_k.shape[2]
    d_ff = w_down.shape[1]
    conv_w = w_conv.shape[1]
    assert 2 * hd == LANES and conv_w == 3 and d % LANES == 0
    alpha = (2.0 * depth) ** 0.25
    qscale = LOG2E * hd ** -0.5
    cw = 256
    assert d_ff % cw == 0
    nch = d_ff // cw
    tb_s = 3 * LANES
    tks_pad = -(-(past + ts) // tb_s) * tb_s
    fox_tile = 256 if tp % 256 == 0 else tp
    sb_rows = 128 if tp % 128 == 0 else tp
    sb_groups = 4 if tp % (4 * sb_rows) == 0 else 1

    xp = x_prompt.reshape(bp * tp, d)
    xs = x_sample.reshape(bs * ts, d)
    tm_p, tm_s = _token_tile(bp * tp), _token_tile(bs * ts)
    tl_p = _token_tile(tp)
    zero_conv = jnp.zeros((bp, 2 * nch, 2, cw), F32)

    def conv_state_in(s):
        return s.reshape(s.shape[0], 2, 2 * nch, cw).transpose(0, 2, 1, 3)

    def conv_state_out(s):
        return s.transpose(0, 2, 1, 3).reshape(s.shape[0], 2, 2 * d_ff)

    def heads(y, b, t):
        return y.reshape(b, t, n_heads, hd)

    outs = {n: [] for n in ("p_a_k", "p_a_v", "p_a_f", "p_b_k", "p_b_v", "p_conv",
                            "s_a_k", "s_a_v", "s_a_f", "s_b_k", "s_b_v", "s_conv")}
    for i in range(depth):
        j = i // 2
        fox = i % 2 == 0
        w_in = w_a_in[j] if fox else w_b_in[j]
        w3 = w_in[:, :3 * d].reshape(d, 3, d).transpose(1, 0, 2).astype(BF16)
        wf = bf = None
        if fox:
            wf = jnp.pad(w_in[:, 3 * d:], ((0, 0), (0, LANES - n_heads))).astype(BF16)
            bf = jnp.pad(b_a_f[j], (0, LANES - n_heads)).reshape(1, LANES)
        w_o = (w_a_o[j] if fox else w_b_o[j]).astype(BF16)
        g1, b1 = ln1_g[i].reshape(1, d), ln1_b[i].reshape(1, d)
        g2, b2 = ln2_g[i].reshape(1, d), ln2_b[i].reshape(1, d)
        wu = w_up[i].reshape(d, 2 * nch, cw).transpose(1, 0, 2).astype(BF16)
        wc = w_conv[i].reshape(conv_w, 2 * nch, cw).transpose(1, 0, 2)
        bc = b_conv[i].reshape(2 * nch, 1, cw)
        wd = w_down[i].reshape(nch, cw, d).astype(BF16)

        res = _inproj(xp, w3, wf, bf, tm=tm_p, qscale=qscale, n_heads=n_heads)
        q, kb, vb, k, v = res[:5]
        q, kb, vb = (a.reshape(bp, tp, d) for a in (q, kb, vb))
        if fox:
            lf = res[5].reshape(bp, tp, n_heads)
            f2 = _cumsum_log2(lf.transpose(0, 2, 1).reshape(bp * n_heads, tp))
            f2 = f2.reshape(bp, n_heads // 2, 2, tp // fox_tile, fox_tile)
            o = _fox_attention(q, kb, vb, f2, tq=fox_tile, tk=fox_tile, q_off=0, hd=hd)
            outs["p_a_k"].append(heads(k, bp, tp)); outs["p_a_v"].append(heads(v, bp, tp))
            outs["p_a_f"].append(lf)
        else:
            o = _sb_attention(q, kb, vb, ng=sb_groups, tr=sb_rows, tb=sb_rows, q_off=0, hd=hd)
            outs["p_b_k"].append(heads(k, bp, tp)); outs["p_b_v"].append(heads(v, bp, tp))
        hp = _outproj_ln(o.reshape(bp * tp, d), xp, w_o, g1, b1, tm=tm_p, alpha=alpha)
        xp, cs = _conv_ffn_ln(hp, zero_conv, wu, wc, bc, wd, g2, b2, ns=1, tl=tl_p, alpha=alpha)
        outs["p_conv"].append(conv_state_out(cs))

        res = _inproj(xs, w3, wf, bf, tm=tm_s, qscale=qscale, n_heads=n_heads)
        q, kb, vb, k, v = res[:5]
        q = q.reshape(bs, ts, d)
        ck, cv = (cache_a_k[j], cache_a_v[j]) if fox else (cache_b_k[j], cache_b_v[j])
        k_all = _pad_time(jnp.concatenate([ck.reshape(bs, past, d).astype(BF16), kb.reshape(bs, ts, d)], 1), tks_pad)
        v_all = _pad_time(jnp.concatenate([cv.reshape(bs, past, d).astype(BF16), vb.reshape(bs, ts, d)], 1), tks_pad)
        if fox:
            lf = res[5].reshape(bs, ts, n_heads)
            lf_all = _pad_time(jnp.concatenate([cache_a_logf[j].astype(F32), lf], 1), tks_pad)
            f2 = _cumsum_log2(lf_all.transpose(0, 2, 1).reshape(bs * n_heads, tks_pad))
            f2 = f2.reshape(bs, n_heads // 2, 2, 1, tks_pad)
            o = _fox_attention(q, k_all, v_all, f2, tq=ts, tk=tks_pad, q_off=past, hd=hd)
            outs["s_a_k"].append(heads(k, bs, ts)); outs["s_a_v"].append(heads(v, bs, ts))
            outs["s_a_f"].append(lf)
        else:
            o = _sb_attention(q, k_all, v_all, ng=1, tr=ts, tb=tb_s, q_off=past, hd=hd)
            outs["s_b_k"].append(heads(k, bs, ts)); outs["s_b_v"].append(heads(v, bs, ts))
        hs = _outproj_ln(o.reshape(bs * ts, d), xs, w_o, g1, b1, tm=tm_s, alpha=alpha)
        xs, cs = _conv_ffn_ln(hs, conv_state_in(state_conv[i]), wu, wc, bc, wd, g2, b2,
                              ns=bs, tl=ts, alpha=alpha)
        outs["s_conv"].append(conv_state_out(cs))

    st = {n: jnp.stack(v) for n, v in outs.items()}
    return (xp.reshape(bp, tp, d), xs.reshape(bs, ts, d),
            st["p_a_k"], st["p_a_v"], st["p_a_f"], st["p_b_k"], st["p_b_v"], st["p_conv"],
            st["s_a_k"], st["s_a_v"], st["s_a_f"], st["s_b_k"], st["s_b_v"], st["s_conv"])
```

```python
import functools

import numpy as np
import jax
import jax.numpy as jnp
from jax import lax
from jax.experimental import pallas as pl
from jax.experimental.pallas import tpu as pltpu

F32 = jnp.float32
BF16 = jnp.bfloat16
LOG2E = 1.4426950408889634
LN_EPS = 1e-5
LANES = 128
NEG_BIG = -1e30
ZERO_LOG2 = 150.0
NORM_SLACK = 1.0 + 2.0 ** -6
EXP_HEADROOM = 90.0
VMEM_LIMIT = 56 * 1024 * 1024
GELU_C = 0.7978845608028654


def _cparams(*sem):
    return pltpu.CompilerParams(dimension_semantics=sem, vmem_limit_bytes=VMEM_LIMIT)


def _resident(shape):
    nd = len(shape)
    return pl.BlockSpec(shape, lambda *_: (0,) * nd, pipeline_mode=pl.Buffered(1))


def _in_head(lane, h, hd):
    return lane < hd if h == 0 else lane >= hd


def _log_sigmoid(x):
    return jnp.minimum(x, 0.0) - jnp.log(1.0 + jnp.exp(-jnp.abs(x)))


def _layer_norm(y, g, b):
    mu = jnp.mean(y, axis=-1, keepdims=True)
    yc = y - mu
    var = jnp.mean(yc * yc, axis=-1, keepdims=True)
    return yc * lax.rsqrt(var + LN_EPS) * g + b


def _inproj_kernel(x_ref, w_ref, *rest, has_f, qscale, n_heads):
    if has_f:
        wf_ref, bf_ref, q_ref, kb_ref, vb_ref, k_ref, v_ref, lf_ref = rest
    else:
        q_ref, kb_ref, vb_ref, k_ref, v_ref = rest
    xb = x_ref[...].astype(BF16)
    tm, d = xb.shape
    hd = d // n_heads

    def split_heads(y, y_ref):
        for h in range(n_heads):
            y_ref[pl.ds(h, tm, stride=n_heads), :] = y[:, h * hd:(h + 1) * hd]

    q = jnp.dot(xb, w_ref[:, 0:d], preferred_element_type=F32)
    q_ref[...] = (q * qscale).astype(BF16)
    k = jnp.dot(xb, w_ref[:, d:2 * d], preferred_element_type=F32)
    kb_ref[...] = k.astype(BF16)
    split_heads(k, k_ref)
    v = jnp.dot(xb, w_ref[:, 2 * d:3 * d], preferred_element_type=F32)
    vb_ref[...] = v.astype(BF16)
    split_heads(v, v_ref)
    if has_f:
        z = jnp.dot(xb, wf_ref[...], preferred_element_type=F32) + bf_ref[...]
        lf_ref[...] = _log_sigmoid(z)[:, :n_heads]


def _inproj(x, w3, wf, bf, *, tm, qscale, n_heads):
    r, d = x.shape
    hd = d // n_heads
    has_f = wf is not None
    row = lambda i: (i, 0)
    in_specs = [pl.BlockSpec((tm, d), row), _resident(w3.shape)]
    args = [x, w3]
    out_shape = [jax.ShapeDtypeStruct((r, d), BF16)] * 3 + [jax.ShapeDtypeStruct((r * n_heads, hd), F32)] * 2
    out_specs = [pl.BlockSpec((tm, d), row)] * 3 + [pl.BlockSpec((tm * n_heads, hd), row)] * 2
    if has_f:
        in_specs += [_resident(wf.shape), _resident(bf.shape)]
        args += [wf, bf]
        out_shape.append(jax.ShapeDtypeStruct((r, n_heads), F32))
        out_specs.append(pl.BlockSpec((tm, n_heads), row))
    return pl.pallas_call(
        functools.partial(_inproj_kernel, has_f=has_f, qscale=qscale, n_heads=n_heads),
        grid=(r // tm,), in_specs=in_specs, out_specs=out_specs, out_shape=out_shape,
        compiler_params=_cparams("parallel"), name="inproj_f" if has_f else "inproj",
    )(*args)


def _cumsum_kernel(x_ref, u_ref, o_ref, carry_ref, *, tc):
    @pl.when(pl.program_id(0) == 0)
    def _():
        carry_ref[...] = jnp.zeros_like(carry_ref)

    x = x_ref[...]
    r = x.shape[0]
    hi = x.astype(BF16)
    r1 = x - hi.astype(F32)
    mid = r1.astype(BF16)
    lo = (r1 - mid.astype(F32)).astype(BF16)
    y3 = jnp.dot(jnp.concatenate([hi, mid, lo], axis=0), u_ref[...], preferred_element_type=F32)
    y = y3[:r] + y3[r:2 * r] + y3[2 * r:]
    c = carry_ref[...]
    o_ref[...] = (y[:, :tc] + jnp.tile(c, (1, tc // LANES))) * LOG2E
    carry_ref[...] = c + y[:, tc:]


def _prefix_matrix(tc):
    j = np.arange(tc)[:, None]
    s = np.arange(tc + LANES)[None, :]
    return jnp.asarray(np.where((s >= tc) | (j <= s), 1.0, 0.0), dtype=BF16)


def _cumsum_log2(x):
    r, t = x.shape
    tc = 512 if t % 512 == 0 else t
    return pl.pallas_call(
        functools.partial(_cumsum_kernel, tc=tc),
        grid=(t // tc,),
        in_specs=[pl.BlockSpec((r, tc), lambda i: (0, i)), _resident((tc, tc + LANES))],
        out_specs=pl.BlockSpec((r, tc), lambda i: (0, i)),
        out_shape=jax.ShapeDtypeStruct((r, t), F32),
        scratch_shapes=[pltpu.VMEM((r, LANES), F32)],
        compiler_params=_cparams("arbitrary"), name="cumsum",
    )(x, _prefix_matrix(tc))


def _head_sq_norms(x, hd):
    xf = x.astype(F32)
    lrow = lax.broadcasted_iota(jnp.int32, (LANES, LANES), 0)
    lcol = lax.broadcasted_iota(jnp.int32, (LANES, LANES), 1)
    ind = jnp.where((lrow >= hd).astype(jnp.int32) == lcol, 1.0, 0.0).astype(BF16)
    n2 = jnp.dot((xf * xf).astype(BF16), ind, preferred_element_type=F32)
    return jnp.max(n2, axis=0, keepdims=True) * NORM_SLACK


def _fox_kernel(q_ref, k_ref, v_ref, f_ref, o_ref, m_scr, acc_scr, kn_scr, fm_scr,
                *, tq, tk, nkb, q_off, hd):
    qi = pl.program_id(2)
    lane1 = lax.broadcasted_iota(jnp.int32, (1, LANES), 1)

    if nkb > 1:
        @pl.when(qi == 0)
        def _stats():
            kn_scr[...] = jnp.zeros_like(kn_scr)
            fm_scr[...] = jnp.zeros_like(fm_scr)

            def body(j, _):
                start = pl.multiple_of(j * tk, tk)
                kn_scr[pl.ds(j, 1), :] = jnp.sqrt(_head_sq_norms(k_ref[0, pl.ds(start, tk), :], hd))
                fmin = [jnp.min(f_ref[0, 0, h, pl.ds(j, 1), :], axis=1, keepdims=True) for h in range(2)]
                fm_scr[pl.ds(j, 1), :] = jnp.where(lane1 == 0, fmin[0], fmin[1])
                return 0

            lax.fori_loop(0, nkb, body, 0)

    q = q_ref[0]
    lane = lax.broadcasted_iota(jnp.int32, (tq, LANES), 1)
    row = lax.broadcasted_iota(jnp.int32, (tq, tk), 0)
    col = lax.broadcasted_iota(jnp.int32, (tq, tk), 1)
    q_start = q_off + qi * tq
    jd = q_start // tk
    dmask = (jd * tk + col) <= (q_start + row)
    qh = [jnp.where(_in_head(lane, h, hd), q, jnp.zeros_like(q)) for h in range(2)]
    cc = [f_ref[0, 0, h, pl.ds(jd, 1), :][:, 0:1] for h in range(2)]

    m_scr[...] = jnp.full(m_scr.shape, NEG_BIG, F32)
    acc_scr[...] = jnp.zeros_like(acc_scr)

    lane_k = lax.broadcasted_iota(jnp.int32, (tk, LANES), 1)
    ones_k = jnp.ones((tk, LANES), BF16)

    reps = tk // LANES

    def step(js, mode):
        blocks = []
        for j in js:
            start = pl.multiple_of(j * tk, tk)
            vblk = v_ref[0, pl.ds(start, tk), :]
            blocks.append((j, k_ref[0, pl.ds(start, tk), :],
                           [jnp.where(_in_head(lane_k, h, hd), vblk, ones_k) for h in range(2)]))
        s = {}
        for b, (j, kblk, _) in enumerate(blocks):
            for h in range(2):
                x = lax.dot_general(qh[h], kblk, (((1,), (1,)), ((), ())), preferred_element_type=F32)
                x = x + (cc[h] - f_ref[0, 0, h, pl.ds(j, 1), :])
                s[b, h] = jnp.where(dmask, x, NEG_BIG) if mode == "diag" else x
        m_old = [m_scr[h] for h in range(2)]
        if mode == "fixed_max":
            p = {c: jnp.exp2(s[c] - jnp.tile(m_old[c[1]], (1, reps))).astype(BF16) for c in s}
            pv = {c: jnp.dot(p[c], blocks[c[0]][2][c[1]], preferred_element_type=F32) for c in s}
            for h in range(2):
                acc_scr[h] += sum(pv[b, h] for b in range(len(blocks)))
            return
        assert len(js) == 1
        m_new = [jnp.maximum(m_old[h], jnp.max(s[0, h], axis=1, keepdims=True)) for h in range(2)]
        p = [jnp.exp2(s[0, h] - jnp.tile(m_new[h], (1, reps))).astype(BF16) for h in range(2)]
        a = [jnp.exp2(m_old[h] - m_new[h]) for h in range(2)]
        for h in range(2):
            m_scr[h] = m_new[h]
        pv = [jnp.dot(p[h], blocks[0][2][h], preferred_element_type=F32) for h in range(2)]
        for h in range(2):
            acc_scr[h] = a[h] * acc_scr[h] + pv[h]

    step([jd], "diag")

    if nkb > 1:
        jrow = lax.broadcasted_iota(jnp.int32, (kn_scr.shape[0], 1), 0)
        jd_f = jd.astype(F32)
        qn = jnp.sqrt(_head_sq_norms(q, hd))
        lo = jd_f
        excess = jnp.float32(NEG_BIG)
        for h in range(2):
            m_min = jnp.min(m_scr[h], axis=0, keepdims=True)[:, 0:1]
            over = qn[:, h:h + 1] * kn_scr[:, h:h + 1] + cc[h] - fm_scr[:, h:h + 1] - m_min
            need = (over > -ZERO_LOG2) & (jrow < jd)
            lo = jnp.minimum(lo, jnp.min(jnp.where(need, jrow.astype(F32), jd_f)))
            excess = jnp.maximum(excess, jnp.max(jnp.where(need, over, NEG_BIG)))
        lo = lo.astype(jnp.int32)

        def fixed_sweep():
            def pair(i, _):
                step([lo + 2 * i, lo + 2 * i + 1], "fixed_max")
                return 0

            lax.fori_loop(0, (jd - lo) // 2, pair, 0)

            @pl.when((jd - lo) % 2 == 1)
            def _():
                step([jd - 1], "fixed_max")

        def running_sweep():
            def body(j, _):
                step([j], "running_max")
                return 0

            lax.fori_loop(lo, jd, body, 0)

        lax.cond(excess <= EXP_HEADROOM, fixed_sweep, running_sweep)

    first = _in_head(lane, 0, hd)
    num = jnp.where(first, acc_scr[0], acc_scr[1])
    den = pltpu.roll(jnp.where(first, acc_scr[1], acc_scr[0]), hd, 1)
    o_ref[0] = (num / den).astype(BF16)


def _fox_attention(q, k, v, f2, *, tq, tk, q_off, hd):
    b, t_q, d = q.shape
    t_k = k.shape[1]
    nkb = t_k // tk
    nkb_pad = -(-nkb // 8) * 8
    assert tk % tq == 0 and q_off % tq == 0
    return pl.pallas_call(
        functools.partial(_fox_kernel, tq=tq, tk=tk, nkb=nkb, q_off=q_off, hd=hd),
        grid=(b, d // LANES, t_q // tq),
        in_specs=[pl.BlockSpec((1, tq, LANES), lambda bi, hp, qi: (bi, qi, hp)),
                  pl.BlockSpec((1, t_k, LANES), lambda bi, hp, qi: (bi, 0, hp)),
                  pl.BlockSpec((1, t_k, LANES), lambda bi, hp, qi: (bi, 0, hp)),
                  pl.BlockSpec((1, 1, 2, nkb, tk), lambda bi, hp, qi: (bi, hp, 0, 0, 0))],
        out_specs=pl.BlockSpec((1, tq, LANES), lambda bi, hp, qi: (bi, qi, hp)),
        out_shape=jax.ShapeDtypeStruct((b, t_q, d), BF16),
        scratch_shapes=[pltpu.VMEM((2, tq, LANES), F32), pltpu.VMEM((2, tq, LANES), F32),
                        pltpu.VMEM((nkb_pad, LANES), F32), pltpu.VMEM((nkb_pad, LANES), F32)],
        compiler_params=_cparams("parallel", "parallel", "arbitrary"), name="fox_attention",
    )(q, k, v, f2)


def _sb_kernel(q_ref, k_ref, v_ref, u_ref, o_ref, cs_scr, acc_scr, *, ng, tr, tb, q_off, hd):
    qi = pl.program_id(2)
    lane = lax.broadcasted_iota(jnp.int32, (tr, LANES), 1)
    row = lax.broadcasted_iota(jnp.int32, (tr, tb), 0)
    col = lax.broadcasted_iota(jnp.int32, (tr, tb), 1)
    u2 = u_ref[...]
    chains = [(g, h) for g in range(ng) for h in range(2)]

    qh, jd, dmask = {}, [], []
    for g in range(ng):
        q = q_ref[0, g * tr:(g + 1) * tr, :]
        q_start = q_off + (qi * ng + g) * tr
        jd.append(q_start // tb)
        dmask.append((jd[g] * tb + col) < (q_start + row))
        for h in range(2):
            qh[g, h] = jnp.where(_in_head(lane, h, hd), q, jnp.zeros_like(q))

    cs_scr[...] = jnp.zeros_like(cs_scr)
    acc_scr[...] = jnp.zeros_like(acc_scr)

    def step(d, masked):
        j = [jd[g] - d for g in range(ng)]
        start = [pl.multiple_of(jnp.maximum(j[g], 0) * tb, tb) for g in range(ng)]
        kblk = [k_ref[0, pl.ds(start[g], tb), :] for g in range(ng)]
        vblk = [v_ref[0, pl.ds(start[g], tb), :] for g in range(ng)]
        pen = [jnp.where(j[g] >= 0, 0.0, -NEG_BIG) for g in range(ng)]
        z = {c: lax.dot_general(qh[c], kblk[c[0]], (((1,), (1,)), ((), ())), preferred_element_type=F32)
             for c in chains}
        sp = {c: jnp.maximum(z[c], 0.0) + jnp.log2(1.0 + jnp.exp2(-jnp.abs(z[c]))) for c in chains}
        w = {}
        for c in chains:
            spm = jnp.where(dmask[c[0]], sp[c], 0.0) if masked else sp[c]
            hi = spm.astype(BF16)
            lo = (spm - hi.astype(F32)).astype(BF16)
            w[c] = jnp.dot(jnp.concatenate([hi, lo], axis=1), u2, preferred_element_type=F32)
        a = {}
        open_sum = None
        for i, c in enumerate(chains):
            cs_old = cs_scr[i]
            x = jnp.exp2(z[c] - sp[c] - w[c][:, :tb] - jnp.tile(cs_old + pen[c[0]], (1, tb // LANES)))
            a[c] = (jnp.where(dmask[c[0]], x, 0.0) if masked else x).astype(BF16)
            cs_new = cs_old + w[c][:, tb:tb + LANES]
            cs_scr[i] = cs_new
            cand = cs_new + jnp.where(j[c[0]] >= 1, 0.0, -NEG_BIG)
            open_sum = cand if open_sum is None else jnp.minimum(open_sum, cand)
        for i, c in enumerate(chains):
            acc_scr[i] += jnp.dot(a[c], vblk[c[0]], preferred_element_type=F32)
        return (jnp.min(open_sum) < ZERO_LOG2).astype(jnp.int32)

    cont = step(jnp.int32(0), True)
    lax.while_loop(lambda c: c[1] > 0, lambda c: (c[0] + 1, step(c[0], False)), (jnp.int32(1), cont))

    for g in range(ng):
        o = jnp.where(_in_head(lane, 0, hd), acc_scr[2 * g], acc_scr[2 * g + 1])
        o_ref[0, g * tr:(g + 1) * tr, :] = o.astype(BF16)


def _suffix_matrix(tb):
    j = (np.arange(2 * tb) % tb)[:, None]
    s = np.arange(tb + LANES)[None, :]
    return jnp.asarray(np.where((s >= tb) | (j > s), 1.0, 0.0), dtype=BF16)


def _sb_attention(q, k, v, *, ng, tr, tb, q_off, hd):
    b, t_q, d = q.shape
    t_k = k.shape[1]
    tq = ng * tr
    assert tb % tr == 0 and q_off % tr == 0
    return pl.pallas_call(
        functools.partial(_sb_kernel, ng=ng, tr=tr, tb=tb, q_off=q_off, hd=hd),
        grid=(b, d // LANES, t_q // tq),
        in_specs=[pl.BlockSpec((1, tq, LANES), lambda bi, hp, qi: (bi, qi, hp)),
                  pl.BlockSpec((1, t_k, LANES), lambda bi, hp, qi: (bi, 0, hp)),
                  pl.BlockSpec((1, t_k, LANES), lambda bi, hp, qi: (bi, 0, hp)),
                  _resident((2 * tb, tb + LANES))],
        out_specs=pl.BlockSpec((1, tq, LANES), lambda bi, hp, qi: (bi, qi, hp)),
        out_shape=jax.ShapeDtypeStruct((b, t_q, d), BF16),
        scratch_shapes=[pltpu.VMEM((2 * ng, tr, LANES), F32), pltpu.VMEM((2 * ng, tr, LANES), F32)],
        compiler_params=_cparams("parallel", "parallel", "arbitrary"), name="sb_attention",
    )(q, k, v, _suffix_matrix(tb))


def _outproj_kernel(o_ref, x_ref, w_ref, g_ref, b_ref, h_ref, *, alpha):
    y = alpha * x_ref[...] + jnp.dot(o_ref[...], w_ref[...], preferred_element_type=F32)
    h_ref[...] = _layer_norm(y, g_ref[...], b_ref[...])


def _outproj_ln(o, x, w, g, b, *, tm, alpha):
    r, d = x.shape
    row = lambda i: (i, 0)
    return pl.pallas_call(
        functools.partial(_outproj_kernel, alpha=alpha),
        grid=(r // tm,),
        in_specs=[pl.BlockSpec((tm, d), row), pl.BlockSpec((tm, d), row),
                  _resident(w.shape), _resident(g.shape), _resident(b.shape)],
        out_specs=pl.BlockSpec((tm, d), row),
        out_shape=jax.ShapeDtypeStruct((r, d), F32),
        compiler_params=_cparams("parallel"), name="outproj_ln",
    )(o, x, w, g, b)


def _ffn_kernel(h_ref, cp_ref, wu_ref, wc_ref, bc_ref, wd_ref, g_ref, b_ref, o_ref, cs_ref,
                sa_scr, sg_scr, act_scr, *, ns, tl, cw, alpha):
    @pl.when(pl.program_id(1) == 0)
    def _():
        cs_ref[...] = cp_ref[...]

    h = h_ref[...]
    hb = h.astype(BF16)
    nch = wd_ref.shape[0] // cw

    def cols(idx):
        return slice(idx * cw, (idx + 1) * cw)

    def up(idx):
        return jnp.dot(hb, wu_ref[:, cols(idx)], preferred_element_type=F32)

    def conv(idx, u, scr):
        for s in range(ns):
            scr[s, 6:8, :] = cs_ref[s, idx]
            scr[s, 8:8 + tl, :] = u[s * tl:(s + 1) * tl]
            cs_ref[s, idx] = u[(s + 1) * tl - 2:(s + 1) * tl, :]
        u1 = jnp.concatenate([scr[s, 7:7 + tl, :] for s in range(ns)], axis=0)
        u2 = jnp.concatenate([scr[s, 6:6 + tl, :] for s in range(ns)], axis=0)
        wc = wc_ref[:, cols(idx)]
        return bc_ref[:, cols(idx)] + wc[0:1] * u2 + wc[1:2] * u1 + wc[2:3] * u

    ua, ug = up(0), up(nch)
    for c in range(nch):
        cur_a, cur_g = ua, ug
        if c + 1 < nch:
            ua, ug = up(c + 1), up(nch + c + 1)
        a = conv(c, cur_a, sa_scr)
        gt = conv(nch + c, cur_g, sg_scr)
        gelu = 0.5 * gt * (1.0 + jnp.tanh(GELU_C * (gt + 0.044715 * (gt * gt * gt))))
        act_scr[:, cols(c)] = (a * gelu).astype(BF16)
    y = jnp.dot(act_scr[...], wd_ref[...], preferred_element_type=F32)
    o_ref[...] = _layer_norm(alpha * h + y, g_ref[...], b_ref[...])


def _conv_ffn_ln(h, conv_prev, wu, wc, bc, wd, g, b, *, ns, tl, alpha):
    r, d = h.shape
    n_streams = conv_prev.shape[0]
    cw = conv_prev.shape[3]
    nb = n_streams // ns
    nt = r // (n_streams * tl)
    assert ns == 1 or nt == 1
    assert tl % 16 == 0
    st_block = (ns,) + conv_prev.shape[1:]
    return pl.pallas_call(
        functools.partial(_ffn_kernel, ns=ns, tl=tl, cw=cw, alpha=alpha),
        grid=(nb, nt),
        in_specs=[pl.BlockSpec((ns * tl, d), lambda bi, ti: (bi * nt + ti, 0)),
                  pl.BlockSpec(st_block, lambda bi, ti: (bi, 0, 0, 0)),
                  _resident(wu.shape), _resident(wc.shape), _resident(bc.shape), _resident(wd.shape),
                  _resident(g.shape), _resident(b.shape)],
        out_specs=[pl.BlockSpec((ns * tl, d), lambda bi, ti: (bi * nt + ti, 0)),
                   pl.BlockSpec(st_block, lambda bi, ti: (bi, 0, 0, 0))],
        out_shape=[jax.ShapeDtypeStruct((r, d), F32), jax.ShapeDtypeStruct(conv_prev.shape, F32)],
        scratch_shapes=[pltpu.VMEM((ns, tl + 8, cw), F32)] * 2 + [pltpu.VMEM((ns * tl, wd.shape[0]), BF16)],
        compiler_params=_cparams("parallel", "arbitrary"), name="conv_ffn_ln",
    )(h, conv_prev, wu, wc, bc, wd, g, b)


def _token_tile(r):
    for tm in (512, 256, 128, 64, 32, 16, 8):
        if r % tm == 0:
            return tm
    raise ValueError(f"row count {r} is not a multiple of 8")


def _pad_time(x, t_pad):
    return jnp.pad(x, ((0, 0), (0, t_pad - x.shape[1])) + ((0, 0),) * (x.ndim - 2))


def kernel(x_prompt, x_sample, cache_a_k, cache_a_v, cache_a_logf, cache_b_k, cache_b_v, state_conv,
           w_a_in, b_a_f, w_a_o, w_b_in, w_b_o, ln1_g, ln1_b, w_up, w_conv, b_conv, w_down, ln2_g, ln2_b):
    bp, tp, d = x_prompt.shape
    bs, ts, _ = x_sample.shape
    depth = ln1_g.shape[0]
    n_heads, hd = cache_a_k.shape[3], cache_a_k.shape[4]
    past = cache_a_k.shape[2]
    d_ff = w_down.shape[1]
    conv_w = w_conv.shape[1]
    assert 2 * hd == LANES and conv_w == 3 and d % LANES == 0
    alpha = (2.0 * depth) ** 0.25
    qscale = LOG2E * hd ** -0.5
    cw = 256
    assert d_ff % cw == 0
    nch = d_ff // cw
    tb_s = 3 * LANES
    tks_pad = -(-(past + ts) // tb_s) * tb_s
    fox_tile = 256 if tp % 256 == 0 else tp
    sb_rows = 128 if tp % 128 == 0 else tp
    sb_groups = 4 if tp % (4 * sb_rows) == 0 else 1

    xp = x_prompt.reshape(bp * tp, d)
    xs = x_sample.reshape(bs * ts, d)
    tm_p, tm_s = _token_tile(bp * tp), _token_tile(bs * ts)
    tl_p = _token_tile(tp)
    zero_conv = jnp.zeros((bp, 2 * nch, 2, cw), F32)

    def conv_state_in(s):
        return s.reshape(s.shape[0], 2, 2 * nch, cw).transpose(0, 2, 1, 3)

    def conv_state_out(s):
        return s.transpose(0, 2, 1, 3).reshape(s.shape[0], 2, 2 * d_ff)

    def heads(y, b, t):
        return y.reshape(b, t, n_heads, hd)

    outs = {n: [] for n in ("p_a_k", "p_a_v", "p_a_f", "p_b_k", "p_b_v", "p_conv",
                            "s_a_k", "s_a_v", "s_a_f", "s_b_k", "s_b_v", "s_conv")}
    for i in range(depth):
        j = i // 2
        fox = i % 2 == 0
        w_in = w_a_in[j] if fox else w_b_in[j]
        w3 = w_in[:, :3 * d].astype(BF16)
        wf = bf = None
        if fox:
            wf = jnp.pad(w_in[:, 3 * d:], ((0, 0), (0, LANES - n_heads))).astype(BF16)
            bf = jnp.pad(b_a_f[j], (0, LANES - n_heads)).reshape(1, LANES)
        w_o = (w_a_o[j] if fox else w_b_o[j]).astype(BF16)
        g1, b1 = ln1_g[i].reshape(1, d), ln1_b[i].reshape(1, d)
        g2, b2 = ln2_g[i].reshape(1, d), ln2_b[i].reshape(1, d)
        wu = w_up[i].astype(BF16)
        wc = w_conv[i]
        bc = b_conv[i].reshape(1, 2 * d_ff)
        wd = w_down[i].astype(BF16)

        res = _inproj(xp, w3, wf, bf, tm=tm_p, qscale=qscale, n_heads=n_heads)
        q, kb, vb, k, v = res[:5]
        q, kb, vb = (a.reshape(bp, tp, d) for a in (q, kb, vb))
        if fox:
            lf = res[5].reshape(bp, tp, n_heads)
            f2 = _cumsum_log2(lf.transpose(0, 2, 1).reshape(bp * n_heads, tp))
            f2 = f2.reshape(bp, n_heads // 2, 2, tp // fox_tile, fox_tile)
            o = _fox_attention(q, kb, vb, f2, tq=fox_tile, tk=fox_tile, q_off=0, hd=hd)
            outs["p_a_k"].append(heads(k, bp, tp)); outs["p_a_v"].append(heads(v, bp, tp))
            outs["p_a_f"].append(lf)
        else:
            o = _sb_attention(q, kb, vb, ng=sb_groups, tr=sb_rows, tb=sb_rows, q_off=0, hd=hd)
            outs["p_b_k"].append(heads(k, bp, tp)); outs["p_b_v"].append(heads(v, bp, tp))
        hp = _outproj_ln(o.reshape(bp * tp, d), xp, w_o, g1, b1, tm=tm_p, alpha=alpha)
        xp, cs = _conv_ffn_ln(hp, zero_conv, wu, wc, bc, wd, g2, b2, ns=1, tl=tl_p, alpha=alpha)
        outs["p_conv"].append(conv_state_out(cs))

        res = _inproj(xs, w3, wf, bf, tm=tm_s, qscale=qscale, n_heads=n_heads)
        q, kb, vb, k, v = res[:5]
        q = q.reshape(bs, ts, d)
        ck, cv = (cache_a_k[j], cache_a_v[j]) if fox else (cache_b_k[j], cache_b_v[j])
        k_all = _pad_time(jnp.concatenate([ck.reshape(bs, past, d).astype(BF16), kb.reshape(bs, ts, d)], 1), tks_pad)
        v_all = _pad_time(jnp.concatenate([cv.reshape(bs, past, d).astype(BF16), vb.reshape(bs, ts, d)], 1), tks_pad)
        if fox:
            lf = res[5].reshape(bs, ts, n_heads)
            lf_all = _pad_time(jnp.concatenate([cache_a_logf[j].astype(F32), lf], 1), tks_pad)
            f2 = _cumsum_log2(lf_all.transpose(0, 2, 1).reshape(bs * n_heads, tks_pad))
            f2 = f2.reshape(bs, n_heads // 2, 2, 1, tks_pad)
            o = _fox_attention(q, k_all, v_all, f2, tq=ts, tk=tks_pad, q_off=past, hd=hd)
            outs["s_a_k"].append(heads(k, bs, ts)); outs["s_a_v"].append(heads(v, bs, ts))
            outs["s_a_f"].append(lf)
        else:
            o = _sb_attention(q, k_all, v_all, ng=1, tr=ts, tb=tb_s, q_off=past, hd=hd)
            outs["s_b_k"].append(heads(k, bs, ts)); outs["s_b_v"].append(heads(v, bs, ts))
        hs = _outproj_ln(o.reshape(bs * ts, d), xs, w_o, g1, b1, tm=tm_s, alpha=alpha)
        xs, cs = _conv_ffn_ln(hs, conv_state_in(state_conv[i]), wu, wc, bc, wd, g2, b2,
                              ns=bs, tl=ts, alpha=alpha)
        outs["s_conv"].append(conv_state_out(cs))

    st = {n: jnp.stack(v) for n, v in outs.items()}
    return (xp.reshape(bp, tp, d), xs.reshape(bs, ts, d),
            st["p_a_k"], st["p_a_v"], st["p_a_f"], st["p_b_k"], st["p_b_v"], st["p_conv"],
            st["s_a_k"], st["s_a_v"], st["s_a_f"], st["s_b_k"], st["s_b_v"], st["s_conv"])
```

```python
import functools

import numpy as np
import jax
import jax.numpy as jnp
from jax import lax
from jax.experimental import pallas as pl
from jax.experimental.pallas import tpu as pltpu

F32 = jnp.float32
BF16 = jnp.bfloat16
LOG2E = 1.4426950408889634
LN_EPS = 1e-5
LANES = 128
NEG_BIG = -1e30
ZERO_LOG2 = 150.0
NORM_SLACK = 1.0 + 2.0 ** -6
EXP_HEADROOM = 90.0
VMEM_LIMIT = 56 * 1024 * 1024
GELU_C = 0.7978845608028654


def _cparams(*sem):
    return pltpu.CompilerParams(dimension_semantics=sem, vmem_limit_bytes=VMEM_LIMIT)


def _resident(shape):
    nd = len(shape)
    return pl.BlockSpec(shape, lambda *_: (0,) * nd, pipeline_mode=pl.Buffered(1))


def _in_head(lane, h, hd):
    return lane < hd if h == 0 else lane >= hd


def _log_sigmoid(x):
    return jnp.minimum(x, 0.0) - jnp.log(1.0 + jnp.exp(-jnp.abs(x)))


def _layer_norm(y, g, b):
    mu = jnp.mean(y, axis=-1, keepdims=True)
    yc = y - mu
    var = jnp.mean(yc * yc, axis=-1, keepdims=True)
    return yc * lax.rsqrt(var + LN_EPS) * g + b


def _cast_kernel(x_ref, o_ref):
    o_ref[...] = x_ref[...].astype(o_ref.dtype)


def _to_bf16(x, layer):
    _, r, c = x.shape
    tr = 256 if r % 256 == 0 else r
    return pl.pallas_call(
        _cast_kernel, grid=(r // tr,),
        in_specs=[pl.BlockSpec((None, tr, c), lambda i: (layer, i, 0))],
        out_specs=pl.BlockSpec((tr, c), lambda i: (i, 0)),
        out_shape=jax.ShapeDtypeStruct((r, c), BF16),
        compiler_params=_cparams("parallel"), name="cast_bf16",
    )(x)


def _inproj_kernel(x_ref, w_ref, *rest, has_f, qscale, n_heads):
    if has_f:
        wf_ref, bf_ref, q_ref, kb_ref, vb_ref, k_ref, v_ref, lf_ref = rest
    else:
        q_ref, kb_ref, vb_ref, k_ref, v_ref = rest
    xb = x_ref[...].astype(BF16)
    tm, d = xb.shape
    hd = d // n_heads

    def split_heads(y, y_ref):
        for h in range(n_heads):
            y_ref[pl.ds(h, tm, stride=n_heads), :] = y[:, h * hd:(h + 1) * hd]

    q = jnp.dot(xb, w_ref[:, 0:d], preferred_element_type=F32)
    q_ref[...] = (q * qscale).astype(BF16)
    k = jnp.dot(xb, w_ref[:, d:2 * d], preferred_element_type=F32)
    kb_ref[...] = k.astype(BF16)
    split_heads(k, k_ref)
    v = jnp.dot(xb, w_ref[:, 2 * d:3 * d], preferred_element_type=F32)
    vb_ref[...] = v.astype(BF16)
    split_heads(v, v_ref)
    if has_f:
        z = jnp.dot(xb, wf_ref[...], preferred_element_type=F32) + bf_ref[...]
        lf_ref[...] = _log_sigmoid(z)[:, :n_heads]


def _inproj(x, w3, wf, bf, *, tm, qscale, n_heads):
    r, d = x.shape
    hd = d // n_heads
    has_f = wf is not None
    row = lambda i: (i, 0)
    in_specs = [pl.BlockSpec((tm, d), row), _resident(w3.shape)]
    args = [x, w3]
    out_shape = [jax.ShapeDtypeStruct((r, d), BF16)] * 3 + [jax.ShapeDtypeStruct((r * n_heads, hd), F32)] * 2
    out_specs = [pl.BlockSpec((tm, d), row)] * 3 + [pl.BlockSpec((tm * n_heads, hd), row)] * 2
    if has_f:
        in_specs += [_resident(wf.shape), _resident(bf.shape)]
        args += [wf, bf]
        out_shape.append(jax.ShapeDtypeStruct((r, n_heads), F32))
        out_specs.append(pl.BlockSpec((tm, n_heads), row))
    return pl.pallas_call(
        functools.partial(_inproj_kernel, has_f=has_f, qscale=qscale, n_heads=n_heads),
        grid=(r // tm,), in_specs=in_specs, out_specs=out_specs, out_shape=out_shape,
        compiler_params=_cparams("parallel"), name="inproj_f" if has_f else "inproj",
    )(*args)


def _cumsum_kernel(x_ref, u_ref, o_ref, carry_ref, *, tc):
    @pl.when(pl.program_id(0) == 0)
    def _():
        carry_ref[...] = jnp.zeros_like(carry_ref)

    x = x_ref[...]
    r = x.shape[0]
    hi = x.astype(BF16)
    r1 = x - hi.astype(F32)
    mid = r1.astype(BF16)
    lo = (r1 - mid.astype(F32)).astype(BF16)
    y3 = jnp.dot(jnp.concatenate([hi, mid, lo], axis=0), u_ref[...], preferred_element_type=F32)
    y = y3[:r] + y3[r:2 * r] + y3[2 * r:]
    c = carry_ref[...]
    o_ref[...] = (y[:, :tc] + jnp.tile(c, (1, tc // LANES))) * LOG2E
    carry_ref[...] = c + y[:, tc:]


def _prefix_matrix(tc):
    j = np.arange(tc)[:, None]
    s = np.arange(tc + LANES)[None, :]
    return jnp.asarray(np.where((s >= tc) | (j <= s), 1.0, 0.0), dtype=BF16)


def _cumsum_log2(x):
    r, t = x.shape
    tc = 512 if t % 512 == 0 else t
    return pl.pallas_call(
        functools.partial(_cumsum_kernel, tc=tc),
        grid=(t // tc,),
        in_specs=[pl.BlockSpec((r, tc), lambda i: (0, i)), _resident((tc, tc + LANES))],
        out_specs=pl.BlockSpec((r, tc), lambda i: (0, i)),
        out_shape=jax.ShapeDtypeStruct((r, t), F32),
        scratch_shapes=[pltpu.VMEM((r, LANES), F32)],
        compiler_params=_cparams("arbitrary"), name="cumsum",
    )(x, _prefix_matrix(tc))


def _head_sq(x, hd):
    xf = x.astype(F32)
    lrow = lax.broadcasted_iota(jnp.int32, (LANES, LANES), 0)
    lcol = lax.broadcasted_iota(jnp.int32, (LANES, LANES), 1)
    ind = jnp.where((lrow >= hd).astype(jnp.int32) == lcol, 1.0, 0.0).astype(BF16)
    return jnp.dot((xf * xf).astype(BF16), ind, preferred_element_type=F32)


def _fox_kernel(q_ref, k_ref, v_ref, f_ref, o_ref, m_scr, acc_scr, kn_scr, fm_scr,
                *, ng, tq, tk, nkb, q_off, hd):
    qi = pl.program_id(2)

    if nkb > 1:
        @pl.when(qi == 0)
        def _stats():
            cb = 8 if nkb % 8 == 0 else nkb
            lane_c = lax.broadcasted_iota(jnp.int32, (cb, LANES), 1)
            kn_scr[...] = jnp.zeros_like(kn_scr)
            fm_scr[...] = jnp.zeros_like(fm_scr)

            def body(c, _):
                j0 = pl.multiple_of(c * cb, cb)
                n2 = _head_sq(k_ref[0, pl.ds(pl.multiple_of(j0 * tk, cb * tk), cb * tk), :], hd)
                kn_scr[pl.ds(j0, cb), :] = jnp.sqrt(jnp.max(n2.reshape(cb, tk, LANES), axis=1) * NORM_SLACK)
                fmin = [jnp.min(f_ref[0, 0, h, pl.ds(j0, cb), :], axis=1, keepdims=True) for h in range(2)]
                fm_scr[pl.ds(j0, cb), :] = jnp.where(lane_c == 0, fmin[0], fmin[1])
                return 0

            lax.fori_loop(0, nkb // cb, body, 0)

    lane = lax.broadcasted_iota(jnp.int32, (tq, LANES), 1)
    row = lax.broadcasted_iota(jnp.int32, (tq, tk), 0)
    col = lax.broadcasted_iota(jnp.int32, (tq, tk), 1)
    lane_k = lax.broadcasted_iota(jnp.int32, (tk, LANES), 1)
    ones_k = jnp.ones((tk, LANES), BF16)
    reps = tk // LANES
    chains = [(g, h) for g in range(ng) for h in range(2)]

    qg, qh, jd, dmask, cc = [], {}, [], [], {}
    for g in range(ng):
        q = q_ref[0, g * tq:(g + 1) * tq, :]
        q_start = q_off + (qi * ng + g) * tq
        qg.append(q)
        jd.append(q_start // tk)
        dmask.append((jd[g] * tk + col) <= (q_start + row))
        for h in range(2):
            qh[g, h] = jnp.where(_in_head(lane, h, hd), q, jnp.zeros_like(q))
            cc[g, h] = f_ref[0, 0, h, pl.ds(jd[g], 1), :][:, 0:1]

    m_scr[...] = jnp.full(m_scr.shape, NEG_BIG, F32)
    acc_scr[...] = jnp.zeros_like(acc_scr)

    def step(ds, mode):
        items = []
        for d in ds:
            for g in range(ng):
                j = jd[g] - d
                jc = jnp.maximum(j, 0)
                start = pl.multiple_of(jc * tk, tk)
                vblk = v_ref[0, pl.ds(start, tk), :]
                items.append((g, jc, jnp.where(j >= 0, 0.0, NEG_BIG),
                              k_ref[0, pl.ds(start, tk), :],
                              [jnp.where(_in_head(lane_k, h, hd), vblk, ones_k) for h in range(2)]))
        s = {}
        for i, (g, jc, pen, kblk, _) in enumerate(items):
            for h in range(2):
                x = lax.dot_general(qh[g, h], kblk, (((1,), (1,)), ((), ())), preferred_element_type=F32)
                x = x + (cc[g, h] + pen - f_ref[0, 0, h, pl.ds(jc, 1), :])
                s[i, h] = jnp.where(dmask[g], x, NEG_BIG) if mode == "diag" else x
        m_old = {c: m_scr[2 * c[0] + c[1]] for c in chains}
        if mode == "fixed_max":
            p = {c: jnp.exp2(s[c] - jnp.tile(m_old[items[c[0]][0], c[1]], (1, reps))).astype(BF16) for c in s}
            pv = {c: jnp.dot(p[c], items[c[0]][4][c[1]], preferred_element_type=F32) for c in s}
            for g, h in chains:
                acc_scr[2 * g + h] += sum(pv[i, h] for i in range(len(items)) if items[i][0] == g)
            return
        assert len(ds) == 1
        m_new = {c: jnp.maximum(m_old[c], jnp.max(s[c], axis=1, keepdims=True)) for c in chains}
        p = {c: jnp.exp2(s[c] - jnp.tile(m_new[c], (1, reps))).astype(BF16) for c in chains}
        a = {c: jnp.exp2(m_old[c] - m_new[c]) for c in chains}
        for g, h in chains:
            m_scr[2 * g + h] = m_new[g, h]
        pv = {c: jnp.dot(p[c], items[c[0]][4][c[1]], preferred_element_type=F32) for c in chains}
        for g, h in chains:
            acc_scr[2 * g + h] = a[g, h] * acc_scr[2 * g + h] + pv[g, h]

    step([0], "diag")

    if nkb > 1:
        jrow = lax.broadcasted_iota(jnp.int32, (kn_scr.shape[0], 1), 0)
        jrow_f = jrow.astype(F32)
        depth = jnp.float32(0.0)
        excess = jnp.float32(NEG_BIG)
        for g in range(ng):
            jd_f = jd[g].astype(F32)
            qn = jnp.sqrt(jnp.max(_head_sq(qg[g], hd), axis=0, keepdims=True) * NORM_SLACK)
            for h in range(2):
                m_min = jnp.min(m_scr[2 * g + h], axis=0, keepdims=True)[:, 0:1]
                over = qn[:, h:h + 1] * kn_scr[:, h:h + 1] + cc[g, h] - fm_scr[:, h:h + 1] - m_min
                need = (over > -ZERO_LOG2) & (jrow < jd[g])
                depth = jnp.maximum(depth, jnp.max(jnp.where(need, jd_f - jrow_f, 0.0)))
                excess = jnp.maximum(excess, jnp.max(jnp.where(need, over, NEG_BIG)))
        depth = depth.astype(jnp.int32)

        def fixed_sweep():
            def pair(i, _):
                step([2 * i + 1, 2 * i + 2], "fixed_max")
                return 0

            lax.fori_loop(0, depth // 2, pair, 0)

            @pl.when(depth % 2 == 1)
            def _():
                step([depth], "fixed_max")

        def running_sweep():
            def body(d, _):
                step([d], "running_max")
                return 0

            lax.fori_loop(1, depth + 1, body, 0)

        lax.cond(excess <= EXP_HEADROOM, fixed_sweep, running_sweep)

    first = _in_head(lane, 0, hd)
    for g in range(ng):
        num = jnp.where(first, acc_scr[2 * g], acc_scr[2 * g + 1])
        den = pltpu.roll(jnp.where(first, acc_scr[2 * g + 1], acc_scr[2 * g]), hd, 1)
        o_ref[0, g * tq:(g + 1) * tq, :] = (num / den).astype(BF16)


def _fox_attention(q, k, v, f2, *, ng, tq, tk, q_off, hd):
    b, t_q, d = q.shape
    t_k = k.shape[1]
    nkb = t_k // tk
    nkb_pad = -(-nkb // 8) * 8
    assert tk % tq == 0 and q_off % tq == 0
    return pl.pallas_call(
        functools.partial(_fox_kernel, ng=ng, tq=tq, tk=tk, nkb=nkb, q_off=q_off, hd=hd),
        grid=(b, d // LANES, t_q // (ng * tq)),
        in_specs=[pl.BlockSpec((1, ng * tq, LANES), lambda bi, hp, qi: (bi, qi, hp)),
                  pl.BlockSpec((1, t_k, LANES), lambda bi, hp, qi: (bi, 0, hp)),
                  pl.BlockSpec((1, t_k, LANES), lambda bi, hp, qi: (bi, 0, hp)),
                  pl.BlockSpec((1, 1, 2, nkb, tk), lambda bi, hp, qi: (bi, hp, 0, 0, 0))],
        out_specs=pl.BlockSpec((1, ng * tq, LANES), lambda bi, hp, qi: (bi, qi, hp)),
        out_shape=jax.ShapeDtypeStruct((b, t_q, d), BF16),
        scratch_shapes=[pltpu.VMEM((2 * ng, tq, LANES), F32), pltpu.VMEM((2 * ng, tq, LANES), F32),
                        pltpu.VMEM((nkb_pad, LANES), F32), pltpu.VMEM((nkb_pad, LANES), F32)],
        compiler_params=_cparams("parallel", "parallel", "arbitrary"), name="fox_attention",
    )(q, k, v, f2)


def _sb_kernel(q_ref, k_ref, v_ref, u_ref, o_ref, cs_scr, acc_scr, *, ng, tr, tb, q_off, hd):
    qi = pl.program_id(2)
    lane = lax.broadcasted_iota(jnp.int32, (tr, LANES), 1)
    row = lax.broadcasted_iota(jnp.int32, (tr, tb), 0)
    col = lax.broadcasted_iota(jnp.int32, (tr, tb), 1)
    u2 = u_ref[...]
    chains = [(g, h) for g in range(ng) for h in range(2)]

    qh, jd, dmask = {}, [], []
    for g in range(ng):
        q = q_ref[0, g * tr:(g + 1) * tr, :]
        q_start = q_off + (qi * ng + g) * tr
        jd.append(q_start // tb)
        dmask.append((jd[g] * tb + col) < (q_start + row))
        for h in range(2):
            qh[g, h] = jnp.where(_in_head(lane, h, hd), q, jnp.zeros_like(q))

    cs_scr[...] = jnp.zeros_like(cs_scr)
    acc_scr[...] = jnp.zeros_like(acc_scr)

    def step(d, masked):
        j = [jd[g] - d for g in range(ng)]
        start = [pl.multiple_of(jnp.maximum(j[g], 0) * tb, tb) for g in range(ng)]
        kblk = [k_ref[0, pl.ds(start[g], tb), :] for g in range(ng)]
        vblk = [v_ref[0, pl.ds(start[g], tb), :] for g in range(ng)]
        pen = [jnp.where(j[g] >= 0, 0.0, -NEG_BIG) for g in range(ng)]
        z = {c: lax.dot_general(qh[c], kblk[c[0]], (((1,), (1,)), ((), ())), preferred_element_type=F32)
             for c in chains}
        sp = {c: jnp.maximum(z[c], 0.0) + jnp.log2(1.0 + jnp.exp2(-jnp.abs(z[c]))) for c in chains}
        w = {}
        for c in chains:
            spm = jnp.where(dmask[c[0]], sp[c], 0.0) if masked else sp[c]
            hi = spm.astype(BF16)
            lo = (spm - hi.astype(F32)).astype(BF16)
            w[c] = jnp.dot(jnp.concatenate([hi, lo], axis=1), u2, preferred_element_type=F32)
        a = {}
        open_sum = None
        for i, c in enumerate(chains):
            cs_old = cs_scr[i]
            x = jnp.exp2(z[c] - sp[c] - w[c][:, :tb] - jnp.tile(cs_old + pen[c[0]], (1, tb // LANES)))
            a[c] = (jnp.where(dmask[c[0]], x, 0.0) if masked else x).astype(BF16)
            cs_new = cs_old + w[c][:, tb:tb + LANES]
            cs_scr[i] = cs_new
            cand = cs_new + jnp.where(j[c[0]] >= 1, 0.0, -NEG_BIG)
            open_sum = cand if open_sum is None else jnp.minimum(open_sum, cand)
        for i, c in enumerate(chains):
            acc_scr[i] += jnp.dot(a[c], vblk[c[0]], preferred_element_type=F32)
        return (jnp.min(open_sum) < ZERO_LOG2).astype(jnp.int32)

    cont = step(jnp.int32(0), True)
    lax.while_loop(lambda c: c[1] > 0, lambda c: (c[0] + 1, step(c[0], False)), (jnp.int32(1), cont))

    for g in range(ng):
        o = jnp.where(_in_head(lane, 0, hd), acc_scr[2 * g], acc_scr[2 * g + 1])
        o_ref[0, g * tr:(g + 1) * tr, :] = o.astype(BF16)


def _suffix_matrix(tb):
    j = (np.arange(2 * tb) % tb)[:, None]
    s = np.arange(tb + LANES)[None, :]
    return jnp.asarray(np.where((s >= tb) | (j > s), 1.0, 0.0), dtype=BF16)


def _sb_attention(q, k, v, *, ng, tr, tb, q_off, hd):
    b, t_q, d = q.shape
    t_k = k.shape[1]
    tq = ng * tr
    assert tb % tr == 0 and q_off % tr == 0
    return pl.pallas_call(
        functools.partial(_sb_kernel, ng=ng, tr=tr, tb=tb, q_off=q_off, hd=hd),
        grid=(b, d // LANES, t_q // tq),
        in_specs=[pl.BlockSpec((1, tq, LANES), lambda bi, hp, qi: (bi, qi, hp)),
                  pl.BlockSpec((1, t_k, LANES), lambda bi, hp, qi: (bi, 0, hp)),
                  pl.BlockSpec((1, t_k, LANES), lambda bi, hp, qi: (bi, 0, hp)),
                  _resident((2 * tb, tb + LANES))],
        out_specs=pl.BlockSpec((1, tq, LANES), lambda bi, hp, qi: (bi, qi, hp)),
        out_shape=jax.ShapeDtypeStruct((b, t_q, d), BF16),
        scratch_shapes=[pltpu.VMEM((2 * ng, tr, LANES), F32), pltpu.VMEM((2 * ng, tr, LANES), F32)],
        compiler_params=_cparams("parallel", "parallel", "arbitrary"), name="sb_attention",
    )(q, k, v, _suffix_matrix(tb))


def _outproj_kernel(o_ref, x_ref, w_ref, g_ref, b_ref, h_ref, *, alpha):
    y = alpha * x_ref[...] + jnp.dot(o_ref[...], w_ref[...], preferred_element_type=F32)
    h_ref[...] = _layer_norm(y, g_ref[...], b_ref[...])


def _outproj_ln(o, x, w, g, b, *, tm, alpha):
    r, d = x.shape
    row = lambda i: (i, 0)
    return pl.pallas_call(
        functools.partial(_outproj_kernel, alpha=alpha),
        grid=(r // tm,),
        in_specs=[pl.BlockSpec((tm, d), row), pl.BlockSpec((tm, d), row),
                  _resident(w.shape), _resident(g.shape), _resident(b.shape)],
        out_specs=pl.BlockSpec((tm, d), row),
        out_shape=jax.ShapeDtypeStruct((r, d), F32),
        compiler_params=_cparams("parallel"), name="outproj_ln",
    )(o, x, w, g, b)


def _ffn_kernel(h_ref, cp_ref, wu_ref, wc_ref, bc_ref, wd_ref, g_ref, b_ref, o_ref, cs_ref,
                sa_scr, sg_scr, act_scr, *, ns, tl, cw, alpha):
    @pl.when(pl.program_id(1) == 0)
    def _():
        cs_ref[...] = cp_ref[...]

    h = h_ref[...]
    hb = h.astype(BF16)
    nch = wd_ref.shape[0] // cw

    def cols(idx):
        return slice(idx * cw, (idx + 1) * cw)

    def up(idx):
        return jnp.dot(hb, wu_ref[:, cols(idx)], preferred_element_type=F32)

    def conv(idx, u, scr):
        for s in range(ns):
            scr[s, 6:8, :] = cs_ref[s, idx]
            scr[s, 8:8 + tl, :] = u[s * tl:(s + 1) * tl]
            cs_ref[s, idx] = u[(s + 1) * tl - 2:(s + 1) * tl, :]
        u1 = jnp.concatenate([scr[s, 7:7 + tl, :] for s in range(ns)], axis=0)
        u2 = jnp.concatenate([scr[s, 6:6 + tl, :] for s in range(ns)], axis=0)
        wc = wc_ref[:, cols(idx)]
        return bc_ref[:, cols(idx)] + wc[0:1] * u2 + wc[1:2] * u1 + wc[2:3] * u

    ua, ug = up(0), up(nch)
    for c in range(nch):
        cur_a, cur_g = ua, ug
        if c + 1 < nch:
            ua, ug = up(c + 1), up(nch + c + 1)
        a = conv(c, cur_a, sa_scr)
        gt = conv(nch + c, cur_g, sg_scr)
        gelu = 0.5 * gt * (1.0 + jnp.tanh(GELU_C * (gt + 0.044715 * (gt * gt * gt))))
        act_scr[:, cols(c)] = (a * gelu).astype(BF16)
    y = jnp.dot(act_scr[...], wd_ref[...], preferred_element_type=F32)
    o_ref[...] = _layer_norm(alpha * h + y, g_ref[...], b_ref[...])


def _conv_ffn_ln(h, conv_prev, wu, wc, bc, wd, g, b, *, ns, tl, alpha):
    r, d = h.shape
    n_streams = conv_prev.shape[0]
    cw = conv_prev.shape[3]
    nb = n_streams // ns
    nt = r // (n_streams * tl)
    assert ns == 1 or nt == 1
    assert tl % 16 == 0
    st_block = (ns,) + conv_prev.shape[1:]
    return pl.pallas_call(
        functools.partial(_ffn_kernel, ns=ns, tl=tl, cw=cw, alpha=alpha),
        grid=(nb, nt),
        in_specs=[pl.BlockSpec((ns * tl, d), lambda bi, ti: (bi * nt + ti, 0)),
                  pl.BlockSpec(st_block, lambda bi, ti: (bi, 0, 0, 0)),
                  _resident(wu.shape), _resident(wc.shape), _resident(bc.shape), _resident(wd.shape),
                  _resident(g.shape), _resident(b.shape)],
        out_specs=[pl.BlockSpec((ns * tl, d), lambda bi, ti: (bi * nt + ti, 0)),
                   pl.BlockSpec(st_block, lambda bi, ti: (bi, 0, 0, 0))],
        out_shape=[jax.ShapeDtypeStruct((r, d), F32), jax.ShapeDtypeStruct(conv_prev.shape, F32)],
        scratch_shapes=[pltpu.VMEM((ns, tl + 8, cw), F32)] * 2 + [pltpu.VMEM((ns * tl, wd.shape[0]), BF16)],
        compiler_params=_cparams("parallel", "arbitrary"), name="conv_ffn_ln",
    )(h, conv_prev, wu, wc, bc, wd, g, b)


def _token_tile(r):
    for tm in (512, 256, 128, 64, 32, 16, 8):
        if r % tm == 0:
            return tm
    raise ValueError(f"row count {r} is not a multiple of 8")


def _pad_time(x, t_pad):
    return jnp.pad(x, ((0, 0), (0, t_pad - x.shape[1])) + ((0, 0),) * (x.ndim - 2))


def kernel(x_prompt, x_sample, cache_a_k, cache_a_v, cache_a_logf, cache_b_k, cache_b_v, state_conv,
           w_a_in, b_a_f, w_a_o, w_b_in, w_b_o, ln1_g, ln1_b, w_up, w_conv, b_conv, w_down, ln2_g, ln2_b):
    bp, tp, d = x_prompt.shape
    bs, ts, _ = x_sample.shape
    depth = ln1_g.shape[0]
    n_heads, hd = cache_a_k.shape[3], cache_a_k.shape[4]
    past = cache_a_k.shape[2]
    d_ff = w_down.shape[1]
    conv_w = w_conv.shape[1]
    assert 2 * hd == LANES and conv_w == 3 and d % LANES == 0
    alpha = (2.0 * depth) ** 0.25
    qscale = LOG2E * hd ** -0.5
    cw = 256
    assert d_ff % cw == 0
    nch = d_ff // cw
    tb_s = 3 * LANES
    tks_pad = -(-(past + ts) // tb_s) * tb_s
    fox_tile = 256 if tp % 256 == 0 else tp
    fox_groups = 2 if tp % (2 * fox_tile) == 0 else 1
    sb_rows = 128 if tp % 128 == 0 else tp
    sb_groups = 4 if tp % (4 * sb_rows) == 0 else 1

    xp = x_prompt.reshape(bp * tp, d)
    xs = x_sample.reshape(bs * ts, d)
    tm_p, tm_s = _token_tile(bp * tp), _token_tile(bs * ts)
    tl_p = _token_tile(tp)
    zero_conv = jnp.zeros((bp, 2 * nch, 2, cw), F32)

    def conv_state_in(s):
        return s.reshape(s.shape[0], 2, 2 * nch, cw).transpose(0, 2, 1, 3)

    def conv_state_out(s):
        return s.transpose(0, 2, 1, 3).reshape(s.shape[0], 2, 2 * d_ff)

    def heads(y, b, t):
        return y.reshape(b, t, n_heads, hd)

    outs = {n: [] for n in ("p_a_k", "p_a_v", "p_a_f", "p_b_k", "p_b_v", "p_conv",
                            "s_a_k", "s_a_v", "s_a_f", "s_b_k", "s_b_v", "s_conv")}
    for i in range(depth):
        j = i // 2
        fox = i % 2 == 0
        w3 = _to_bf16(w_a_in if fox else w_b_in, j)
        wf = bf = None
        if fox:
            wf = jnp.pad(w_a_in[j, :, 3 * d:], ((0, 0), (0, LANES - n_heads))).astype(BF16)
            bf = jnp.pad(b_a_f[j], (0, LANES - n_heads)).reshape(1, LANES)
        w_o = _to_bf16(w_a_o if fox else w_b_o, j)
        g1, b1 = ln1_g[i].reshape(1, d), ln1_b[i].reshape(1, d)
        g2, b2 = ln2_g[i].reshape(1, d), ln2_b[i].reshape(1, d)
        wu = _to_bf16(w_up, i)
        wc = w_conv[i]
        bc = b_conv[i].reshape(1, 2 * d_ff)
        wd = _to_bf16(w_down, i)

        res = _inproj(xp, w3, wf, bf, tm=tm_p, qscale=qscale, n_heads=n_heads)
        q, kb, vb, k, v = res[:5]
        q, kb, vb = (a.reshape(bp, tp, d) for a in (q, kb, vb))
        if fox:
            lf = res[5].reshape(bp, tp, n_heads)
            f2 = _cumsum_log2(lf.transpose(0, 2, 1).reshape(bp * n_heads, tp))
            f2 = f2.reshape(bp, n_heads // 2, 2, tp // fox_tile, fox_tile)
            o = _fox_attention(q, kb, vb, f2, ng=fox_groups, tq=fox_tile, tk=fox_tile, q_off=0, hd=hd)
            outs["p_a_k"].append(heads(k, bp, tp)); outs["p_a_v"].append(heads(v, bp, tp))
            outs["p_a_f"].append(lf)
        else:
            o = _sb_attention(q, kb, vb, ng=sb_groups, tr=sb_rows, tb=sb_rows, q_off=0, hd=hd)
            outs["p_b_k"].append(heads(k, bp, tp)); outs["p_b_v"].append(heads(v, bp, tp))
        hp = _outproj_ln(o.reshape(bp * tp, d), xp, w_o, g1, b1, tm=tm_p, alpha=alpha)
        xp, cs = _conv_ffn_ln(hp, zero_conv, wu, wc, bc, wd, g2, b2, ns=1, tl=tl_p, alpha=alpha)
        outs["p_conv"].append(conv_state_out(cs))

        res = _inproj(xs, w3, wf, bf, tm=tm_s, qscale=qscale, n_heads=n_heads)
        q, kb, vb, k, v = res[:5]
        q = q.reshape(bs, ts, d)
        ck, cv = (cache_a_k[j], cache_a_v[j]) if fox else (cache_b_k[j], cache_b_v[j])
        k_all = _pad_time(jnp.concatenate([ck.reshape(bs, past, d).astype(BF16), kb.reshape(bs, ts, d)], 1), tks_pad)
        v_all = _pad_time(jnp.concatenate([cv.reshape(bs, past, d).astype(BF16), vb.reshape(bs, ts, d)], 1), tks_pad)
        if fox:
            lf = res[5].reshape(bs, ts, n_heads)
            lf_all = _pad_time(jnp.concatenate([cache_a_logf[j].astype(F32), lf], 1), tks_pad)
            f2 = _cumsum_log2(lf_all.transpose(0, 2, 1).reshape(bs * n_heads, tks_pad))
            f2 = f2.reshape(bs, n_heads // 2, 2, 1, tks_pad)
            o = _fox_attention(q, k_all, v_all, f2, ng=1, tq=ts, tk=tks_pad, q_off=past, hd=hd)
            outs["s_a_k"].append(heads(k, bs, ts)); outs["s_a_v"].append(heads(v, bs, ts))
            outs["s_a_f"].append(lf)
        else:
            o = _sb_attention(q, k_all, v_all, ng=1, tr=ts, tb=tb_s, q_off=past, hd=hd)
            outs["s_b_k"].append(heads(k, bs, ts)); outs["s_b_v"].append(heads(v, bs, ts))
        hs = _outproj_ln(o.reshape(bs * ts, d), xs, w_o, g1, b1, tm=tm_s, alpha=alpha)
        xs, cs = _conv_ffn_ln(hs, conv_state_in(state_conv[i]), wu, wc, bc, wd, g2, b2,
                              ns=bs, tl=ts, alpha=alpha)
        outs["s_conv"].append(conv_state_out(cs))

    st = {n: jnp.stack(v) for n, v in outs.items()}
    return (xp.reshape(bp, tp, d), xs.reshape(bs, ts, d),
            st["p_a_k"], st["p_a_v"], st["p_a_f"], st["p_b_k"], st["p_b_v"], st["p_conv"],
            st["s_a_k"], st["s_a_v"], st["s_a_f"], st["s_b_k"], st["s_b_v"], st["s_conv"])
```

```python
import functools

import numpy as np
import jax
import jax.numpy as jnp
from jax import lax
from jax.experimental import pallas as pl
from jax.experimental.pallas import tpu as pltpu

F32 = jnp.float32
BF16 = jnp.bfloat16
LOG2E = 1.4426950408889634
LN_EPS = 1e-5
LANES = 128
NEG_BIG = -1e30
ZERO_LOG2 = 150.0
NORM_SLACK = 1.0 + 2.0 ** -6
EXP_HEADROOM = 90.0
VMEM_LIMIT = 56 * 1024 * 1024
GELU_C = 0.7978845608028654


def _cparams(*sem):
    return pltpu.CompilerParams(dimension_semantics=sem, vmem_limit_bytes=VMEM_LIMIT)


def _resident(shape):
    nd = len(shape)
    return pl.BlockSpec(shape, lambda *_: (0,) * nd, pipeline_mode=pl.Buffered(1))


def _in_head(lane, h, hd):
    return lane < hd if h == 0 else lane >= hd


def _log_sigmoid(x):
    return jnp.minimum(x, 0.0) - jnp.log(1.0 + jnp.exp(-jnp.abs(x)))


def _layer_norm(y, g, b):
    mu = jnp.mean(y, axis=-1, keepdims=True)
    yc = y - mu
    var = jnp.mean(yc * yc, axis=-1, keepdims=True)
    return yc * lax.rsqrt(var + LN_EPS) * g + b


def _cast_kernel(x_ref, o_ref):
    o_ref[...] = x_ref[...].astype(o_ref.dtype)


def _to_bf16(x, layer):
    _, r, c = x.shape
    tr = 256 if r % 256 == 0 else r
    return pl.pallas_call(
        _cast_kernel, grid=(r // tr,),
        in_specs=[pl.BlockSpec((None, tr, c), lambda i: (layer, i, 0))],
        out_specs=pl.BlockSpec((tr, c), lambda i: (i, 0)),
        out_shape=jax.ShapeDtypeStruct((r, c), BF16),
        compiler_params=_cparams("parallel"), name="cast_bf16",
    )(x)


def _inproj_kernel(x_ref, w_ref, *rest, has_f, qscale, n_heads):
    if has_f:
        wf_ref, bf_ref, q_ref, kb_ref, vb_ref, k_ref, v_ref, lf_ref = rest
    else:
        q_ref, kb_ref, vb_ref, k_ref, v_ref = rest
    xb = x_ref[...].astype(BF16)
    tm, d = xb.shape
    hd = d // n_heads

    def split_heads(y, y_ref):
        for h in range(n_heads):
            y_ref[pl.ds(h, tm, stride=n_heads), :] = y[:, h * hd:(h + 1) * hd]

    q = jnp.dot(xb, w_ref[:, 0:d], preferred_element_type=F32)
    q_ref[...] = (q * qscale).astype(BF16)
    k = jnp.dot(xb, w_ref[:, d:2 * d], preferred_element_type=F32)
    kb_ref[...] = k.astype(BF16)
    split_heads(k, k_ref)
    v = jnp.dot(xb, w_ref[:, 2 * d:3 * d], preferred_element_type=F32)
    vb_ref[...] = v.astype(BF16)
    split_heads(v, v_ref)
    if has_f:
        z = jnp.dot(xb, wf_ref[...], preferred_element_type=F32) + bf_ref[...]
        lf_ref[...] = _log_sigmoid(z)[:, :n_heads]


def _inproj(x, w3, wf, bf, *, tm, qscale, n_heads):
    r, d = x.shape
    hd = d // n_heads
    has_f = wf is not None
    row = lambda i: (i, 0)
    in_specs = [pl.BlockSpec((tm, d), row), _resident(w3.shape)]
    args = [x, w3]
    out_shape = [jax.ShapeDtypeStruct((r, d), BF16)] * 3 + [jax.ShapeDtypeStruct((r * n_heads, hd), F32)] * 2
    out_specs = [pl.BlockSpec((tm, d), row)] * 3 + [pl.BlockSpec((tm * n_heads, hd), row)] * 2
    if has_f:
        in_specs += [_resident(wf.shape), _resident(bf.shape)]
        args += [wf, bf]
        out_shape.append(jax.ShapeDtypeStruct((r, n_heads), F32))
        out_specs.append(pl.BlockSpec((tm, n_heads), row))
    return pl.pallas_call(
        functools.partial(_inproj_kernel, has_f=has_f, qscale=qscale, n_heads=n_heads),
        grid=(r // tm,), in_specs=in_specs, out_specs=out_specs, out_shape=out_shape,
        compiler_params=_cparams("parallel"), name="inproj_f" if has_f else "inproj",
    )(*args)


def _cumsum_kernel(x_ref, u_ref, o_ref, carry_ref, *, tc):
    @pl.when(pl.program_id(0) == 0)
    def _():
        carry_ref[...] = jnp.zeros_like(carry_ref)

    x = x_ref[...]
    r = x.shape[0]
    hi = x.astype(BF16)
    r1 = x - hi.astype(F32)
    mid = r1.astype(BF16)
    lo = (r1 - mid.astype(F32)).astype(BF16)
    y3 = jnp.dot(jnp.concatenate([hi, mid, lo], axis=0), u_ref[...], preferred_element_type=F32)
    y = y3[:r] + y3[r:2 * r] + y3[2 * r:]
    c = carry_ref[...]
    o_ref[...] = (y[:, :tc] + jnp.tile(c, (1, tc // LANES))) * LOG2E
    carry_ref[...] = c + y[:, tc:]


def _prefix_matrix(tc):
    j = np.arange(tc)[:, None]
    s = np.arange(tc + LANES)[None, :]
    return jnp.asarray(np.where((s >= tc) | (j <= s), 1.0, 0.0), dtype=BF16)


def _cumsum_log2(x):
    r, t = x.shape
    tc = 512 if t % 512 == 0 else t
    return pl.pallas_call(
        functools.partial(_cumsum_kernel, tc=tc),
        grid=(t // tc,),
        in_specs=[pl.BlockSpec((r, tc), lambda i: (0, i)), _resident((tc, tc + LANES))],
        out_specs=pl.BlockSpec((r, tc), lambda i: (0, i)),
        out_shape=jax.ShapeDtypeStruct((r, t), F32),
        scratch_shapes=[pltpu.VMEM((r, LANES), F32)],
        compiler_params=_cparams("arbitrary"), name="cumsum",
    )(x, _prefix_matrix(tc))


def _head_sq(x, hd):
    xf = x.astype(F32)
    lrow = lax.broadcasted_iota(jnp.int32, (LANES, LANES), 0)
    lcol = lax.broadcasted_iota(jnp.int32, (LANES, LANES), 1)
    ind = jnp.where((lrow >= hd).astype(jnp.int32) == lcol, 1.0, 0.0).astype(BF16)
    return jnp.dot((xf * xf).astype(BF16), ind, preferred_element_type=F32)


def _fox_kernel(q_ref, k_ref, v_ref, f_ref, o_ref, m_scr, acc_scr, kn_scr, fm_scr,
                *, ng, tq, tk, nkb, q_off, hd):
    qi = pl.program_id(2)

    if nkb > 1:
        @pl.when(qi == 0)
        def _stats():
            cb = 8 if nkb % 8 == 0 else nkb
            lane_c = lax.broadcasted_iota(jnp.int32, (cb, LANES), 1)
            kn_scr[...] = jnp.zeros_like(kn_scr)
            fm_scr[...] = jnp.zeros_like(fm_scr)

            def body(c, _):
                j0 = pl.multiple_of(c * cb, cb)
                n2 = _head_sq(k_ref[0, pl.ds(pl.multiple_of(j0 * tk, cb * tk), cb * tk), :], hd)
                kn_scr[pl.ds(j0, cb), :] = jnp.sqrt(jnp.max(n2.reshape(cb, tk, LANES), axis=1) * NORM_SLACK)
                fmin = [jnp.min(f_ref[0, 0, h, pl.ds(j0, cb), :], axis=1, keepdims=True) for h in range(2)]
                fm_scr[pl.ds(j0, cb), :] = jnp.where(lane_c == 0, fmin[0], fmin[1])
                return 0

            lax.fori_loop(0, nkb // cb, body, 0)

    lane = lax.broadcasted_iota(jnp.int32, (tq, LANES), 1)
    row = lax.broadcasted_iota(jnp.int32, (tq, tk), 0)
    col = lax.broadcasted_iota(jnp.int32, (tq, tk), 1)
    lane_k = lax.broadcasted_iota(jnp.int32, (tk, LANES), 1)
    ones_k = jnp.ones((tk, LANES), BF16)
    reps = tk // LANES
    chains = [(g, h) for g in range(ng) for h in range(2)]

    qg, qh, jd, dmask, cc = [], {}, [], [], {}
    for g in range(ng):
        q = q_ref[0, g * tq:(g + 1) * tq, :]
        q_start = q_off + (qi * ng + g) * tq
        qg.append(q)
        jd.append(q_start // tk)
        dmask.append((jd[g] * tk + col) <= (q_start + row))
        for h in range(2):
            qh[g, h] = jnp.where(_in_head(lane, h, hd), q, jnp.zeros_like(q))
            cc[g, h] = f_ref[0, 0, h, pl.ds(jd[g], 1), :][:, 0:1]

    m_scr[...] = jnp.full(m_scr.shape, NEG_BIG, F32)
    acc_scr[...] = jnp.zeros_like(acc_scr)

    def step(ds, mode):
        items = []
        for d in ds:
            for g in range(ng):
                j = jd[g] - d
                jc = jnp.maximum(j, 0)
                start = pl.multiple_of(jc * tk, tk)
                vblk = v_ref[0, pl.ds(start, tk), :]
                items.append((g, jc, jnp.where(j >= 0, 0.0, NEG_BIG),
                              k_ref[0, pl.ds(start, tk), :],
                              [jnp.where(_in_head(lane_k, h, hd), vblk, ones_k) for h in range(2)]))
        s = {}
        for i, (g, jc, pen, kblk, _) in enumerate(items):
            for h in range(2):
                x = lax.dot_general(qh[g, h], kblk, (((1,), (1,)), ((), ())), preferred_element_type=F32)
                x = x + (cc[g, h] + pen - f_ref[0, 0, h, pl.ds(jc, 1), :])
                s[i, h] = jnp.where(dmask[g], x, NEG_BIG) if mode == "diag" else x
        m_old = {c: m_scr[2 * c[0] + c[1]] for c in chains}
        if mode == "fixed_max":
            p = {c: jnp.exp2(s[c] - jnp.tile(m_old[items[c[0]][0], c[1]], (1, reps))).astype(BF16) for c in s}
            pv = {c: jnp.dot(p[c], items[c[0]][4][c[1]], preferred_element_type=F32) for c in s}
            for g, h in chains:
                acc_scr[2 * g + h] += sum(pv[i, h] for i in range(len(items)) if items[i][0] == g)
            return
        assert len(ds) == 1
        m_new = {c: jnp.maximum(m_old[c], jnp.max(s[c], axis=1, keepdims=True)) for c in chains}
        p = {c: jnp.exp2(s[c] - jnp.tile(m_new[c], (1, reps))).astype(BF16) for c in chains}
        a = {c: jnp.exp2(m_old[c] - m_new[c]) for c in chains}
        for g, h in chains:
            m_scr[2 * g + h] = m_new[g, h]
        pv = {c: jnp.dot(p[c], items[c[0]][4][c[1]], preferred_element_type=F32) for c in chains}
        for g, h in chains:
            acc_scr[2 * g + h] = a[g, h] * acc_scr[2 * g + h] + pv[g, h]

    step([0], "diag")

    if nkb > 1:
        jrow = lax.broadcasted_iota(jnp.int32, (kn_scr.shape[0], 1), 0)
        jrow_f = jrow.astype(F32)
        depth = jnp.float32(0.0)
        excess = jnp.float32(NEG_BIG)
        for g in range(ng):
            jd_f = jd[g].astype(F32)
            qn = jnp.sqrt(jnp.max(_head_sq(qg[g], hd), axis=0, keepdims=True) * NORM_SLACK)
            for h in range(2):
                m_min = jnp.min(m_scr[2 * g + h], axis=0, keepdims=True)[:, 0:1]
                over = qn[:, h:h + 1] * kn_scr[:, h:h + 1] + cc[g, h] - fm_scr[:, h:h + 1] - m_min
                need = (over > -ZERO_LOG2) & (jrow < jd[g])
                depth = jnp.maximum(depth, jnp.max(jnp.where(need, jd_f - jrow_f, 0.0)))
                excess = jnp.maximum(excess, jnp.max(jnp.where(need, over, NEG_BIG)))
        depth = depth.astype(jnp.int32)

        def fixed_sweep():
            def pair(i, _):
                step([2 * i + 1, 2 * i + 2], "fixed_max")
                return 0

            lax.fori_loop(0, depth // 2, pair, 0)

            @pl.when(depth % 2 == 1)
            def _():
                step([depth], "fixed_max")

        def running_sweep():
            def body(d, _):
                step([d], "running_max")
                return 0

            lax.fori_loop(1, depth + 1, body, 0)

        lax.cond(excess <= EXP_HEADROOM, fixed_sweep, running_sweep)

    first = _in_head(lane, 0, hd)
    for g in range(ng):
        num = jnp.where(first, acc_scr[2 * g], acc_scr[2 * g + 1])
        den = pltpu.roll(jnp.where(first, acc_scr[2 * g + 1], acc_scr[2 * g]), hd, 1)
        o_ref[0, g * tq:(g + 1) * tq, :] = (num / den).astype(BF16)


def _fox_attention(q, k, v, f2, *, ng, tq, tk, q_off, hd):
    b, t_q, d = q.shape
    t_k = k.shape[1]
    nkb = t_k // tk
    nkb_pad = -(-nkb // 8) * 8
    assert tk % tq == 0 and q_off % tq == 0
    return pl.pallas_call(
        functools.partial(_fox_kernel, ng=ng, tq=tq, tk=tk, nkb=nkb, q_off=q_off, hd=hd),
        grid=(b, d // LANES, t_q // (ng * tq)),
        in_specs=[pl.BlockSpec((1, ng * tq, LANES), lambda bi, hp, qi: (bi, qi, hp)),
                  pl.BlockSpec((1, t_k, LANES), lambda bi, hp, qi: (bi, 0, hp)),
                  pl.BlockSpec((1, t_k, LANES), lambda bi, hp, qi: (bi, 0, hp)),
                  pl.BlockSpec((1, 1, 2, nkb, tk), lambda bi, hp, qi: (bi, hp, 0, 0, 0))],
        out_specs=pl.BlockSpec((1, ng * tq, LANES), lambda bi, hp, qi: (bi, qi, hp)),
        out_shape=jax.ShapeDtypeStruct((b, t_q, d), BF16),
        scratch_shapes=[pltpu.VMEM((2 * ng, tq, LANES), F32), pltpu.VMEM((2 * ng, tq, LANES), F32),
                        pltpu.VMEM((nkb_pad, LANES), F32), pltpu.VMEM((nkb_pad, LANES), F32)],
        compiler_params=_cparams("parallel", "parallel", "arbitrary"), name="fox_attention",
    )(q, k, v, f2)


def _sb_kernel(q_ref, k_ref, v_ref, u_ref, o_ref, cs_scr, acc_scr, *, ng, tr, tb, q_off, hd, first):
    qi = pl.program_id(2)
    lane = lax.broadcasted_iota(jnp.int32, (tr, LANES), 1)
    row = lax.broadcasted_iota(jnp.int32, (tr, tb), 0)
    col = lax.broadcasted_iota(jnp.int32, (tr, tb), 1)
    u2 = u_ref[...]
    chains = [(g, h) for g in range(ng) for h in range(2)]

    qh, jd, dmask = {}, [], []
    for g in range(ng):
        q = q_ref[0, g * tr:(g + 1) * tr, :]
        q_start = q_off + (qi * ng + g) * tr
        jd.append(q_start // tb)
        dmask.append((jd[g] * tb + col) < (q_start + row))
        for h in range(2):
            qh[g, h] = jnp.where(_in_head(lane, h, hd), q, jnp.zeros_like(q))

    cs_scr[...] = jnp.zeros_like(cs_scr)
    acc_scr[...] = jnp.zeros_like(acc_scr)

    def step(ds, diag_first):
        j, kblk, vblk, pen = {}, {}, {}, {}
        for k, d in enumerate(ds):
            for g in range(ng):
                j[k, g] = jd[g] - d
                start = pl.multiple_of(jnp.maximum(j[k, g], 0) * tb, tb)
                kblk[k, g] = k_ref[0, pl.ds(start, tb), :]
                vblk[k, g] = v_ref[0, pl.ds(start, tb), :]
                pen[k, g] = jnp.where(j[k, g] >= 0, 0.0, -NEG_BIG)
        links = [(k, g, h) for k in range(len(ds)) for g, h in chains]
        masked = {c: diag_first and c[0] == 0 for c in links}
        z = {c: lax.dot_general(qh[c[1], c[2]], kblk[c[0], c[1]], (((1,), (1,)), ((), ())),
                                preferred_element_type=F32) for c in links}
        sp = {c: jnp.maximum(z[c], 0.0) + jnp.log2(1.0 + jnp.exp2(-jnp.abs(z[c]))) for c in links}
        w = {}
        for c in links:
            spm = jnp.where(dmask[c[1]], sp[c], 0.0) if masked[c] else sp[c]
            hi = spm.astype(BF16)
            lo = (spm - hi.astype(F32)).astype(BF16)
            w[c] = jnp.dot(jnp.concatenate([hi, lo], axis=1), u2, preferred_element_type=F32)
        a = {}
        open_sum = None
        for i, (g, h) in enumerate(chains):
            cs = cs_scr[i]
            for k in range(len(ds)):
                c = (k, g, h)
                x = jnp.exp2(z[c] - sp[c] - w[c][:, :tb] - jnp.tile(cs + pen[k, g], (1, tb // LANES)))
                a[c] = (jnp.where(dmask[g], x, 0.0) if masked[c] else x).astype(BF16)
                cs = cs + w[c][:, tb:tb + LANES]
            cs_scr[i] = cs
            cand = cs + jnp.where(j[len(ds) - 1, g] >= 1, 0.0, -NEG_BIG)
            open_sum = cand if open_sum is None else jnp.minimum(open_sum, cand)
        for i, (g, h) in enumerate(chains):
            acc_scr[i] += sum(jnp.dot(a[k, g, h], vblk[k, g], preferred_element_type=F32)
                              for k in range(len(ds)))
        return (jnp.min(open_sum) < ZERO_LOG2).astype(jnp.int32)

    cont = step(list(range(first)), True)
    lax.while_loop(lambda c: c[1] > 0, lambda c: (c[0] + 1, step([c[0]], False)), (jnp.int32(first), cont))

    for g in range(ng):
        o = jnp.where(_in_head(lane, 0, hd), acc_scr[2 * g], acc_scr[2 * g + 1])
        o_ref[0, g * tr:(g + 1) * tr, :] = o.astype(BF16)


def _suffix_matrix(tb):
    j = (np.arange(2 * tb) % tb)[:, None]
    s = np.arange(tb + LANES)[None, :]
    return jnp.asarray(np.where((s >= tb) | (j > s), 1.0, 0.0), dtype=BF16)


def _sb_attention(q, k, v, *, ng, tr, tb, q_off, hd, first):
    b, t_q, d = q.shape
    t_k = k.shape[1]
    tq = ng * tr
    assert tb % tr == 0 and q_off % tr == 0
    return pl.pallas_call(
        functools.partial(_sb_kernel, ng=ng, tr=tr, tb=tb, q_off=q_off, hd=hd, first=first),
        grid=(b, d // LANES, t_q // tq),
        in_specs=[pl.BlockSpec((1, tq, LANES), lambda bi, hp, qi: (bi, qi, hp)),
                  pl.BlockSpec((1, t_k, LANES), lambda bi, hp, qi: (bi, 0, hp)),
                  pl.BlockSpec((1, t_k, LANES), lambda bi, hp, qi: (bi, 0, hp)),
                  _resident((2 * tb, tb + LANES))],
        out_specs=pl.BlockSpec((1, tq, LANES), lambda bi, hp, qi: (bi, qi, hp)),
        out_shape=jax.ShapeDtypeStruct((b, t_q, d), BF16),
        scratch_shapes=[pltpu.VMEM((2 * ng, tr, LANES), F32), pltpu.VMEM((2 * ng, tr, LANES), F32)],
        compiler_params=_cparams("parallel", "parallel", "arbitrary"), name="sb_attention",
    )(q, k, v, _suffix_matrix(tb))


def _ffn_kernel(a_ref, x_ref, cp_ref, wo_ref, g1_ref, b1_ref, wu_ref, wc_ref, bc_ref, wd_ref, g_ref, b_ref,
                o_ref, cs_ref, sa_scr, sg_scr, act_scr, *, ns, tl, cw, alpha):
    @pl.when(pl.program_id(1) == 0)
    def _():
        cs_ref[...] = cp_ref[...]

    h = _layer_norm(alpha * x_ref[...] + jnp.dot(a_ref[...], wo_ref[...], preferred_element_type=F32),
                    g1_ref[...], b1_ref[...])
    hb = h.astype(BF16)
    nch = wd_ref.shape[0] // cw

    def cols(idx):
        return slice(idx * cw, (idx + 1) * cw)

    def up(idx):
        return jnp.dot(hb, wu_ref[:, cols(idx)], preferred_element_type=F32)

    def conv(idx, u, scr):
        for s in range(ns):
            scr[s, 6:8, :] = cs_ref[s, idx]
            scr[s, 8:8 + tl, :] = u[s * tl:(s + 1) * tl]
            cs_ref[s, idx] = u[(s + 1) * tl - 2:(s + 1) * tl, :]
        u1 = jnp.concatenate([scr[s, 7:7 + tl, :] for s in range(ns)], axis=0)
        u2 = jnp.concatenate([scr[s, 6:6 + tl, :] for s in range(ns)], axis=0)
        wc = wc_ref[:, cols(idx)]
        return bc_ref[:, cols(idx)] + wc[0:1] * u2 + wc[1:2] * u1 + wc[2:3] * u

    ua, ug = up(0), up(nch)
    for c in range(nch):
        cur_a, cur_g = ua, ug
        if c + 1 < nch:
            ua, ug = up(c + 1), up(nch + c + 1)
        a = conv(c, cur_a, sa_scr)
        gt = conv(nch + c, cur_g, sg_scr)
        gelu = 0.5 * gt * (1.0 + jnp.tanh(GELU_C * (gt + 0.044715 * (gt * gt * gt))))
        act_scr[:, cols(c)] = (a * gelu).astype(BF16)
    y = jnp.dot(act_scr[...], wd_ref[...], preferred_element_type=F32)
    o_ref[...] = _layer_norm(alpha * h + y, g_ref[...], b_ref[...])


def _post_attention(attn, x, conv_prev, wo, g1, b1, wu, wc, bc, wd, g2, b2, *, ns, tl, alpha):
    r, d = x.shape
    n_streams = conv_prev.shape[0]
    cw = conv_prev.shape[3]
    nb = n_streams // ns
    nt = r // (n_streams * tl)
    assert ns == 1 or nt == 1
    assert tl % 16 == 0
    st_block = (ns,) + conv_prev.shape[1:]
    tile = pl.BlockSpec((ns * tl, d), lambda bi, ti: (bi * nt + ti, 0))
    state = pl.BlockSpec(st_block, lambda bi, ti: (bi, 0, 0, 0))
    weights = (wo, g1, b1, wu, wc, bc, wd, g2, b2)
    return pl.pallas_call(
        functools.partial(_ffn_kernel, ns=ns, tl=tl, cw=cw, alpha=alpha),
        grid=(nb, nt),
        in_specs=[tile, tile, state] + [_resident(w.shape) for w in weights],
        out_specs=[tile, state],
        out_shape=[jax.ShapeDtypeStruct((r, d), F32), jax.ShapeDtypeStruct(conv_prev.shape, F32)],
        scratch_shapes=[pltpu.VMEM((ns, tl + 8, cw), F32)] * 2 + [pltpu.VMEM((ns * tl, wd.shape[0]), BF16)],
        compiler_params=_cparams("parallel", "arbitrary"), name="post_attention",
    )(attn, x, conv_prev, *weights)


def _token_tile(r):
    for tm in (512, 256, 128, 64, 32, 16, 8):
        if r % tm == 0:
            return tm
    raise ValueError(f"row count {r} is not a multiple of 8")


def _pad_time(x, t_pad):
    return jnp.pad(x, ((0, 0), (0, t_pad - x.shape[1])) + ((0, 0),) * (x.ndim - 2))


def kernel(x_prompt, x_sample, cache_a_k, cache_a_v, cache_a_logf, cache_b_k, cache_b_v, state_conv,
           w_a_in, b_a_f, w_a_o, w_b_in, w_b_o, ln1_g, ln1_b, w_up, w_conv, b_conv, w_down, ln2_g, ln2_b):
    bp, tp, d = x_prompt.shape
    bs, ts, _ = x_sample.shape
    depth = ln1_g.shape[0]
    n_heads, hd = cache_a_k.shape[3], cache_a_k.shape[4]
    past = cache_a_k.shape[2]
    d_ff = w_down.shape[1]
    conv_w = w_conv.shape[1]
    assert 2 * hd == LANES and conv_w == 3 and d % LANES == 0
    alpha = (2.0 * depth) ** 0.25
    qscale = LOG2E * hd ** -0.5
    cw = 256
    assert d_ff % cw == 0
    nch = d_ff // cw
    tb_s = 3 * LANES
    tks_pad = -(-(past + ts) // tb_s) * tb_s
    fox_tile = 256 if tp % 256 == 0 else tp
    fox_groups = 2 if tp % (2 * fox_tile) == 0 else 1
    sb_rows = 128 if tp % 128 == 0 else tp
    sb_groups = 4 if tp % (4 * sb_rows) == 0 else 1

    xp = x_prompt.reshape(bp * tp, d)
    xs = x_sample.reshape(bs * ts, d)
    tm_p, tm_s = _token_tile(bp * tp), _token_tile(bs * ts)
    tl_p = _token_tile(tp)
    zero_conv = jnp.zeros((bp, 2 * nch, 2, cw), F32)

    def conv_state_in(s):
        return s.reshape(s.shape[0], 2, 2 * nch, cw).transpose(0, 2, 1, 3)

    def conv_state_out(s):
        return s.transpose(0, 2, 1, 3).reshape(s.shape[0], 2, 2 * d_ff)

    def heads(y, b, t):
        return y.reshape(b, t, n_heads, hd)

    outs = {n: [] for n in ("p_a_k", "p_a_v", "p_a_f", "p_b_k", "p_b_v", "p_conv",
                            "s_a_k", "s_a_v", "s_a_f", "s_b_k", "s_b_v", "s_conv")}
    for i in range(depth):
        j = i // 2
        fox = i % 2 == 0
        w3 = _to_bf16(w_a_in if fox else w_b_in, j)
        wf = bf = None
        if fox:
            wf = jnp.pad(w_a_in[j, :, 3 * d:], ((0, 0), (0, LANES - n_heads))).astype(BF16)
            bf = jnp.pad(b_a_f[j], (0, LANES - n_heads)).reshape(1, LANES)
        w_o = _to_bf16(w_a_o if fox else w_b_o, j)
        g1, b1 = ln1_g[i].reshape(1, d), ln1_b[i].reshape(1, d)
        g2, b2 = ln2_g[i].reshape(1, d), ln2_b[i].reshape(1, d)
        wu = _to_bf16(w_up, i)
        wc = w_conv[i]
        bc = b_conv[i].reshape(1, 2 * d_ff)
        wd = _to_bf16(w_down, i)

        res = _inproj(xp, w3, wf, bf, tm=tm_p, qscale=qscale, n_heads=n_heads)
        q, kb, vb, k, v = res[:5]
        q, kb, vb = (a.reshape(bp, tp, d) for a in (q, kb, vb))
        if fox:
            lf = res[5].reshape(bp, tp, n_heads)
            f2 = _cumsum_log2(lf.transpose(0, 2, 1).reshape(bp * n_heads, tp))
            f2 = f2.reshape(bp, n_heads // 2, 2, tp // fox_tile, fox_tile)
            o = _fox_attention(q, kb, vb, f2, ng=fox_groups, tq=fox_tile, tk=fox_tile, q_off=0, hd=hd)
            outs["p_a_k"].append(heads(k, bp, tp)); outs["p_a_v"].append(heads(v, bp, tp))
            outs["p_a_f"].append(lf)
        else:
            o = _sb_attention(q, kb, vb, ng=sb_groups, tr=sb_rows, tb=sb_rows, q_off=0, hd=hd, first=3)
            outs["p_b_k"].append(heads(k, bp, tp)); outs["p_b_v"].append(heads(v, bp, tp))
        xp, cs = _post_attention(o.reshape(bp * tp, d), xp, zero_conv, w_o, g1, b1, wu, wc, bc, wd, g2, b2,
                                 ns=1, tl=tl_p, alpha=alpha)
        outs["p_conv"].append(conv_state_out(cs))

        res = _inproj(xs, w3, wf, bf, tm=tm_s, qscale=qscale, n_heads=n_heads)
        q, kb, vb, k, v = res[:5]
        q = q.reshape(bs, ts, d)
        ck, cv = (cache_a_k[j], cache_a_v[j]) if fox else (cache_b_k[j], cache_b_v[j])
        k_all = _pad_time(jnp.concatenate([ck.reshape(bs, past, d).astype(BF16), kb.reshape(bs, ts, d)], 1), tks_pad)
        v_all = _pad_time(jnp.concatenate([cv.reshape(bs, past, d).astype(BF16), vb.reshape(bs, ts, d)], 1), tks_pad)
        if fox:
            lf = res[5].reshape(bs, ts, n_heads)
            lf_all = _pad_time(jnp.concatenate([cache_a_logf[j].astype(F32), lf], 1), tks_pad)
            f2 = _cumsum_log2(lf_all.transpose(0, 2, 1).reshape(bs * n_heads, tks_pad))
            f2 = f2.reshape(bs, n_heads // 2, 2, 1, tks_pad)
            o = _fox_attention(q, k_all, v_all, f2, ng=1, tq=ts, tk=tks_pad, q_off=past, hd=hd)
            outs["s_a_k"].append(heads(k, bs, ts)); outs["s_a_v"].append(heads(v, bs, ts))
            outs["s_a_f"].append(lf)
        else:
            o = _sb_attention(q, k_all, v_all, ng=1, tr=ts, tb=tb_s, q_off=past, hd=hd, first=1)
            outs["s_b_k"].append(heads(k, bs, ts)); outs["s_b_v"].append(heads(v, bs, ts))
        xs, cs = _post_attention(o.reshape(bs * ts, d), xs, conv_state_in(state_conv[i]), w_o, g1, b1,
                                 wu, wc, bc, wd, g2, b2, ns=bs, tl=ts, alpha=alpha)
        outs["s_conv"].append(conv_state_out(cs))

    st = {n: jnp.stack(v) for n, v in outs.items()}
    return (xp.reshape(bp, tp, d), xs.reshape(bs, ts, d),
            st["p_a_k"], st["p_a_v"], st["p_a_f"], st["p_b_k"], st["p_b_v"], st["p_conv"],
            st["s_a_k"], st["s_a_v"], st["s_a_f"], st["s_b_k"], st["s_b_v"], st["s_conv"])
```

```python
import functools

import numpy as np
import jax
import jax.numpy as jnp
from jax import lax
from jax.experimental import pallas as pl
from jax.experimental.pallas import tpu as pltpu

F32 = jnp.float32
BF16 = jnp.bfloat16
LOG2E = 1.4426950408889634
LN_EPS = 1e-5
LANES = 128
NEG_BIG = -1e30
ZERO_LOG2 = 150.0
NORM_SLACK = 1.0 + 2.0 ** -6
EXP_HEADROOM = 90.0
VMEM_LIMIT = 56 * 1024 * 1024
GELU_C = 0.7978845608028654


def _cparams(*sem):
    return pltpu.CompilerParams(dimension_semantics=sem, vmem_limit_bytes=VMEM_LIMIT)


def _resident(shape):
    nd = len(shape)
    return pl.BlockSpec(shape, lambda *_: (0,) * nd, pipeline_mode=pl.Buffered(1))


def _in_head(lane, h, hd):
    return lane < hd if h == 0 else lane >= hd


def _log_sigmoid(x):
    return jnp.minimum(x, 0.0) - jnp.log(1.0 + jnp.exp(-jnp.abs(x)))


def _layer_norm(y, g, b):
    mu = jnp.mean(y, axis=-1, keepdims=True)
    yc = y - mu
    var = jnp.mean(yc * yc, axis=-1, keepdims=True)
    return yc * lax.rsqrt(var + LN_EPS) * g + b


def _cast_kernel(x_ref, o_ref):
    o_ref[...] = x_ref[...].astype(o_ref.dtype)


def _to_bf16(x, layer):
    _, r, c = x.shape
    tr = 256 if r % 256 == 0 else r
    return pl.pallas_call(
        _cast_kernel, grid=(r // tr,),
        in_specs=[pl.BlockSpec((None, tr, c), lambda i: (layer, i, 0))],
        out_specs=pl.BlockSpec((tr, c), lambda i: (i, 0)),
        out_shape=jax.ShapeDtypeStruct((r, c), BF16),
        compiler_params=_cparams("parallel"), name="cast_bf16",
    )(x)


def _inproj_kernel(x_ref, w_ref, *rest, has_f, qscale, n_heads):
    if has_f:
        wf_ref, bf_ref, q_ref, kb_ref, vb_ref, k_ref, v_ref, lf_ref = rest
    else:
        q_ref, kb_ref, vb_ref, k_ref, v_ref = rest
    xb = x_ref[...].astype(BF16)
    tm, d = xb.shape
    hd = d // n_heads

    def split_heads(y, y_ref):
        for h in range(n_heads):
            y_ref[pl.ds(h, tm, stride=n_heads), :] = y[:, h * hd:(h + 1) * hd]

    k = jnp.dot(xb, w_ref[:, d:2 * d], preferred_element_type=F32)
    v = jnp.dot(xb, w_ref[:, 2 * d:3 * d], preferred_element_type=F32)
    kb_ref[...] = k.astype(BF16)
    split_heads(k, k_ref)
    q = jnp.dot(xb, w_ref[:, 0:d], preferred_element_type=F32)
    vb_ref[...] = v.astype(BF16)
    split_heads(v, v_ref)
    q_ref[...] = (q * qscale).astype(BF16)
    if has_f:
        z = jnp.dot(xb, wf_ref[...], preferred_element_type=F32) + bf_ref[...]
        lf_ref[...] = _log_sigmoid(z)[:, :n_heads]


def _inproj(x, w3, wf, bf, *, tm, qscale, n_heads):
    r, d = x.shape
    hd = d // n_heads
    has_f = wf is not None
    row = lambda i: (i, 0)
    in_specs = [pl.BlockSpec((tm, d), row), _resident(w3.shape)]
    args = [x, w3]
    out_shape = [jax.ShapeDtypeStruct((r, d), BF16)] * 3 + [jax.ShapeDtypeStruct((r * n_heads, hd), F32)] * 2
    out_specs = [pl.BlockSpec((tm, d), row)] * 3 + [pl.BlockSpec((tm * n_heads, hd), row)] * 2
    if has_f:
        in_specs += [_resident(wf.shape), _resident(bf.shape)]
        args += [wf, bf]
        out_shape.append(jax.ShapeDtypeStruct((r, n_heads), F32))
        out_specs.append(pl.BlockSpec((tm, n_heads), row))
    return pl.pallas_call(
        functools.partial(_inproj_kernel, has_f=has_f, qscale=qscale, n_heads=n_heads),
        grid=(r // tm,), in_specs=in_specs, out_specs=out_specs, out_shape=out_shape,
        compiler_params=_cparams("parallel"), name="inproj_f" if has_f else "inproj",
    )(*args)


def _cumsum_kernel(x_ref, u_ref, o_ref, carry_ref, *, tc):
    @pl.when(pl.program_id(0) == 0)
    def _():
        carry_ref[...] = jnp.zeros_like(carry_ref)

    x = x_ref[...]
    r = x.shape[0]
    hi = x.astype(BF16)
    r1 = x - hi.astype(F32)
    mid = r1.astype(BF16)
    lo = (r1 - mid.astype(F32)).astype(BF16)
    y3 = jnp.dot(jnp.concatenate([hi, mid, lo], axis=0), u_ref[...], preferred_element_type=F32)
    y = y3[:r] + y3[r:2 * r] + y3[2 * r:]
    c = carry_ref[...]
    o_ref[...] = (y[:, :tc] + jnp.tile(c, (1, tc // LANES))) * LOG2E
    carry_ref[...] = c + y[:, tc:]


def _prefix_matrix(tc):
    j = np.arange(tc)[:, None]
    s = np.arange(tc + LANES)[None, :]
    return jnp.asarray(np.where((s >= tc) | (j <= s), 1.0, 0.0), dtype=BF16)


def _cumsum_log2(x):
    r, t = x.shape
    tc = 512 if t % 512 == 0 else t
    return pl.pallas_call(
        functools.partial(_cumsum_kernel, tc=tc),
        grid=(t // tc,),
        in_specs=[pl.BlockSpec((r, tc), lambda i: (0, i)), _resident((tc, tc + LANES))],
        out_specs=pl.BlockSpec((r, tc), lambda i: (0, i)),
        out_shape=jax.ShapeDtypeStruct((r, t), F32),
        scratch_shapes=[pltpu.VMEM((r, LANES), F32)],
        compiler_params=_cparams("arbitrary"), name="cumsum",
    )(x, _prefix_matrix(tc))


def _head_sq(x, hd):
    xf = x.astype(F32)
    lrow = lax.broadcasted_iota(jnp.int32, (LANES, LANES), 0)
    lcol = lax.broadcasted_iota(jnp.int32, (LANES, LANES), 1)
    ind = jnp.where((lrow >= hd).astype(jnp.int32) == lcol, 1.0, 0.0).astype(BF16)
    return jnp.dot((xf * xf).astype(BF16), ind, preferred_element_type=F32)


def _fox_kernel(q_ref, k_ref, v_ref, f_ref, o_ref, m_scr, acc_scr, kn_scr, fm_scr,
                *, ng, tq, tk, nkb, q_off, hd):
    qi = pl.program_id(2)

    if nkb > 1:
        @pl.when(qi == 0)
        def _stats():
            cb = 8 if nkb % 8 == 0 else nkb
            lane_c = lax.broadcasted_iota(jnp.int32, (cb, LANES), 1)
            kn_scr[...] = jnp.zeros_like(kn_scr)
            fm_scr[...] = jnp.zeros_like(fm_scr)

            def body(c, _):
                j0 = pl.multiple_of(c * cb, cb)
                n2 = _head_sq(k_ref[0, pl.ds(pl.multiple_of(j0 * tk, cb * tk), cb * tk), :], hd)
                kn_scr[pl.ds(j0, cb), :] = jnp.sqrt(jnp.max(n2.reshape(cb, tk, LANES), axis=1) * NORM_SLACK)
                fmin = [jnp.min(f_ref[0, 0, h, pl.ds(j0, cb), :], axis=1, keepdims=True) for h in range(2)]
                fm_scr[pl.ds(j0, cb), :] = jnp.where(lane_c == 0, fmin[0], fmin[1])
                return 0

            lax.fori_loop(0, nkb // cb, body, 0)

    lane = lax.broadcasted_iota(jnp.int32, (tq, LANES), 1)
    row = lax.broadcasted_iota(jnp.int32, (tq, tk), 0)
    col = lax.broadcasted_iota(jnp.int32, (tq, tk), 1)
    lane_k = lax.broadcasted_iota(jnp.int32, (tk, LANES), 1)
    ones_k = jnp.ones((tk, LANES), BF16)
    reps = tk // LANES
    chains = [(g, h) for g in range(ng) for h in range(2)]

    qg, qh, jd, dmask, cc = [], {}, [], [], {}
    for g in range(ng):
        q = q_ref[0, g * tq:(g + 1) * tq, :]
        q_start = q_off + (qi * ng + g) * tq
        qg.append(q)
        jd.append(q_start // tk)
        dmask.append((jd[g] * tk + col) <= (q_start + row))
        for h in range(2):
            qh[g, h] = jnp.where(_in_head(lane, h, hd), q, jnp.zeros_like(q))
            cc[g, h] = f_ref[0, 0, h, pl.ds(jd[g], 1), :][:, 0:1]

    m_scr[...] = jnp.full(m_scr.shape, NEG_BIG, F32)
    acc_scr[...] = jnp.zeros_like(acc_scr)

    def step(ds, mode):
        items = []
        for d in ds:
            for g in range(ng):
                j = jd[g] - d
                jc = jnp.maximum(j, 0)
                start = pl.multiple_of(jc * tk, tk)
                vblk = v_ref[0, pl.ds(start, tk), :]
                items.append((g, jc, jnp.where(j >= 0, 0.0, NEG_BIG),
                              k_ref[0, pl.ds(start, tk), :],
                              [jnp.where(_in_head(lane_k, h, hd), vblk, ones_k) for h in range(2)]))
        s = {}
        for i, (g, jc, pen, kblk, _) in enumerate(items):
            for h in range(2):
                x = lax.dot_general(qh[g, h], kblk, (((1,), (1,)), ((), ())), preferred_element_type=F32)
                x = x + (cc[g, h] + pen - f_ref[0, 0, h, pl.ds(jc, 1), :])
                s[i, h] = jnp.where(dmask[g], x, NEG_BIG) if mode == "diag" else x
        m_old = {c: m_scr[2 * c[0] + c[1]] for c in chains}
        if mode == "fixed_max":
            p = {c: jnp.exp2(s[c] - jnp.tile(m_old[items[c[0]][0], c[1]], (1, reps))).astype(BF16) for c in s}
            pv = {c: jnp.dot(p[c], items[c[0]][4][c[1]], preferred_element_type=F32) for c in s}
            for g, h in chains:
                acc_scr[2 * g + h] += sum(pv[i, h] for i in range(len(items)) if items[i][0] == g)
            return
        assert len(ds) == 1
        m_new = {c: jnp.maximum(m_old[c], jnp.max(s[c], axis=1, keepdims=True)) for c in chains}
        p = {c: jnp.exp2(s[c] - jnp.tile(m_new[c], (1, reps))).astype(BF16) for c in chains}
        a = {c: jnp.exp2(m_old[c] - m_new[c]) for c in chains}
        for g, h in chains:
            m_scr[2 * g + h] = m_new[g, h]
        pv = {c: jnp.dot(p[c], items[c[0]][4][c[1]], preferred_element_type=F32) for c in chains}
        for g, h in chains:
            acc_scr[2 * g + h] = a[g, h] * acc_scr[2 * g + h] + pv[g, h]

    step([0], "diag")

    if nkb > 1:
        jrow = lax.broadcasted_iota(jnp.int32, (kn_scr.shape[0], 1), 0)
        jrow_f = jrow.astype(F32)
        depth = jnp.float32(0.0)
        excess = jnp.float32(NEG_BIG)
        for g in range(ng):
            jd_f = jd[g].astype(F32)
            qn = jnp.sqrt(jnp.max(_head_sq(qg[g], hd), axis=0, keepdims=True) * NORM_SLACK)
            for h in range(2):
                m_min = jnp.min(m_scr[2 * g + h], axis=0, keepdims=True)[:, 0:1]
                over = qn[:, h:h + 1] * kn_scr[:, h:h + 1] + cc[g, h] - fm_scr[:, h:h + 1] - m_min
                need = (over > -ZERO_LOG2) & (jrow < jd[g])
                depth = jnp.maximum(depth, jnp.max(jnp.where(need, jd_f - jrow_f, 0.0)))
                excess = jnp.maximum(excess, jnp.max(jnp.where(need, over, NEG_BIG)))
        depth = depth.astype(jnp.int32)

        def fixed_sweep():
            def pair(i, _):
                step([2 * i + 1, 2 * i + 2], "fixed_max")
                return 0

            lax.fori_loop(0, depth // 2, pair, 0)

            @pl.when(depth % 2 == 1)
            def _():
                step([depth], "fixed_max")

        def running_sweep():
            def body(d, _):
                step([d], "running_max")
                return 0

            lax.fori_loop(1, depth + 1, body, 0)

        lax.cond(excess <= EXP_HEADROOM, fixed_sweep, running_sweep)

    first = _in_head(lane, 0, hd)
    for g in range(ng):
        num = jnp.where(first, acc_scr[2 * g], acc_scr[2 * g + 1])
        den = pltpu.roll(jnp.where(first, acc_scr[2 * g + 1], acc_scr[2 * g]), hd, 1)
        o_ref[0, g * tq:(g + 1) * tq, :] = (num / den).astype(BF16)


def _fox_attention(q, k, v, f2, *, ng, tq, tk, q_off, hd):
    b, t_q, d = q.shape
    t_k = k.shape[1]
    nkb = t_k // tk
    nkb_pad = -(-nkb // 8) * 8
    assert tk % tq == 0 and q_off % tq == 0
    return pl.pallas_call(
        functools.partial(_fox_kernel, ng=ng, tq=tq, tk=tk, nkb=nkb, q_off=q_off, hd=hd),
        grid=(b, d // LANES, t_q // (ng * tq)),
        in_specs=[pl.BlockSpec((1, ng * tq, LANES), lambda bi, hp, qi: (bi, qi, hp)),
                  pl.BlockSpec((1, t_k, LANES), lambda bi, hp, qi: (bi, 0, hp)),
                  pl.BlockSpec((1, t_k, LANES), lambda bi, hp, qi: (bi, 0, hp)),
                  pl.BlockSpec((1, 1, 2, nkb, tk), lambda bi, hp, qi: (bi, hp, 0, 0, 0))],
        out_specs=pl.BlockSpec((1, ng * tq, LANES), lambda bi, hp, qi: (bi, qi, hp)),
        out_shape=jax.ShapeDtypeStruct((b, t_q, d), BF16),
        scratch_shapes=[pltpu.VMEM((2 * ng, tq, LANES), F32), pltpu.VMEM((2 * ng, tq, LANES), F32),
                        pltpu.VMEM((nkb_pad, LANES), F32), pltpu.VMEM((nkb_pad, LANES), F32)],
        compiler_params=_cparams("parallel", "parallel", "arbitrary"), name="fox_attention",
    )(q, k, v, f2)


def _sb_kernel(q_ref, k_ref, v_ref, u_ref, o_ref, cs_scr, acc_scr, *, ng, tr, tb, q_off, hd, first):
    qi = pl.program_id(2)
    lane = lax.broadcasted_iota(jnp.int32, (tr, LANES), 1)
    row = lax.broadcasted_iota(jnp.int32, (tr, tb), 0)
    col = lax.broadcasted_iota(jnp.int32, (tr, tb), 1)
    u2 = u_ref[...]
    chains = [(g, h) for g in range(ng) for h in range(2)]

    qh, jd, dmask = {}, [], []
    for g in range(ng):
        q = q_ref[0, g * tr:(g + 1) * tr, :]
        q_start = q_off + (qi * ng + g) * tr
        jd.append(q_start // tb)
        dmask.append((jd[g] * tb + col) < (q_start + row))
        for h in range(2):
            qh[g, h] = jnp.where(_in_head(lane, h, hd), q, jnp.zeros_like(q))

    cs_scr[...] = jnp.zeros_like(cs_scr)
    acc_scr[...] = jnp.zeros_like(acc_scr)

    def step(ds, diag_first):
        j, kblk, vblk, pen = {}, {}, {}, {}
        for k, d in enumerate(ds):
            for g in range(ng):
                j[k, g] = jd[g] - d
                start = pl.multiple_of(jnp.maximum(j[k, g], 0) * tb, tb)
                kblk[k, g] = k_ref[0, pl.ds(start, tb), :]
                vblk[k, g] = v_ref[0, pl.ds(start, tb), :]
                pen[k, g] = jnp.where(j[k, g] >= 0, 0.0, -NEG_BIG)
        links = [(k, g, h) for k in range(len(ds)) for g, h in chains]
        masked = {c: diag_first and c[0] == 0 for c in links}
        z = {c: lax.dot_general(qh[c[1], c[2]], kblk[c[0], c[1]], (((1,), (1,)), ((), ())),
                                preferred_element_type=F32) for c in links}
        sp = {c: jnp.maximum(z[c], 0.0) + jnp.log2(1.0 + jnp.exp2(-jnp.abs(z[c]))) for c in links}
        w = {}
        for c in links:
            spm = jnp.where(dmask[c[1]], sp[c], 0.0) if masked[c] else sp[c]
            hi = spm.astype(BF16)
            lo = (spm - hi.astype(F32)).astype(BF16)
            w[c] = jnp.dot(jnp.concatenate([hi, lo], axis=1), u2, preferred_element_type=F32)
        a = {}
        open_sum = None
        for i, (g, h) in enumerate(chains):
            cs = cs_scr[i]
            for k in range(len(ds)):
                c = (k, g, h)
                x = jnp.exp2(z[c] - sp[c] - w[c][:, :tb] - jnp.tile(cs + pen[k, g], (1, tb // LANES)))
                a[c] = (jnp.where(dmask[g], x, 0.0) if masked[c] else x).astype(BF16)
                cs = cs + w[c][:, tb:tb + LANES]
            cs_scr[i] = cs
            cand = cs + jnp.where(j[len(ds) - 1, g] >= 1, 0.0, -NEG_BIG)
            open_sum = cand if open_sum is None else jnp.minimum(open_sum, cand)
        for i, (g, h) in enumerate(chains):
            acc_scr[i] += sum(jnp.dot(a[k, g, h], vblk[k, g], preferred_element_type=F32)
                              for k in range(len(ds)))
        return (jnp.min(open_sum) < ZERO_LOG2).astype(jnp.int32)

    cont = step(list(range(first)), True)
    lax.while_loop(lambda c: c[1] > 0, lambda c: (c[0] + 1, step([c[0]], False)), (jnp.int32(first), cont))

    for g in range(ng):
        o = jnp.where(_in_head(lane, 0, hd), acc_scr[2 * g], acc_scr[2 * g + 1])
        o_ref[0, g * tr:(g + 1) * tr, :] = o.astype(BF16)


def _suffix_matrix(tb):
    j = (np.arange(2 * tb) % tb)[:, None]
    s = np.arange(tb + LANES)[None, :]
    return jnp.asarray(np.where((s >= tb) | (j > s), 1.0, 0.0), dtype=BF16)


def _sb_attention(q, k, v, *, ng, tr, tb, q_off, hd, first):
    b, t_q, d = q.shape
    t_k = k.shape[1]
    tq = ng * tr
    assert tb % tr == 0 and q_off % tr == 0
    return pl.pallas_call(
        functools.partial(_sb_kernel, ng=ng, tr=tr, tb=tb, q_off=q_off, hd=hd, first=first),
        grid=(b, d // LANES, t_q // tq),
        in_specs=[pl.BlockSpec((1, tq, LANES), lambda bi, hp, qi: (bi, qi, hp)),
                  pl.BlockSpec((1, t_k, LANES), lambda bi, hp, qi: (bi, 0, hp)),
                  pl.BlockSpec((1, t_k, LANES), lambda bi, hp, qi: (bi, 0, hp)),
                  _resident((2 * tb, tb + LANES))],
        out_specs=pl.BlockSpec((1, tq, LANES), lambda bi, hp, qi: (bi, qi, hp)),
        out_shape=jax.ShapeDtypeStruct((b, t_q, d), BF16),
        scratch_shapes=[pltpu.VMEM((2 * ng, tr, LANES), F32), pltpu.VMEM((2 * ng, tr, LANES), F32)],
        compiler_params=_cparams("parallel", "parallel", "arbitrary"), name="sb_attention",
    )(q, k, v, _suffix_matrix(tb))


def _ffn_kernel(a_ref, x_ref, cp_ref, wo_ref, g1_ref, b1_ref, wu_ref, wc_ref, bc_ref, wd_ref, g_ref, b_ref,
                o_ref, cs_ref, act_scr, *, ns, tl, cw, alpha):
    @pl.when(pl.program_id(1) == 0)
    def _():
        cs_ref[...] = cp_ref[...]

    h = _layer_norm(alpha * x_ref[...] + jnp.dot(a_ref[...], wo_ref[...], preferred_element_type=F32),
                    g1_ref[...], b1_ref[...])
    hb = h.astype(BF16)
    nch = wd_ref.shape[0] // cw

    def cols(idx):
        return slice(idx * cw, (idx + 1) * cw)

    def up(idx):
        return jnp.dot(hb, wu_ref[:, cols(idx)], preferred_element_type=F32)

    sub = lax.broadcasted_iota(jnp.int32, (1, 8, cw), 1)

    def conv(idx, u):
        p1, p2 = [], []
        for s in range(ns):
            us = u[s * tl:(s + 1) * tl].reshape(tl // 8, 8, cw)
            c0 = cs_ref[s, idx, 0:1, :].reshape(1, 1, cw)
            c1 = cs_ref[s, idx, 1:2, :].reshape(1, 1, cw)
            r1, r2 = pltpu.roll(us, 1, 1), pltpu.roll(us, 2, 1)
            before1 = jnp.concatenate([jnp.broadcast_to(c1, (1, 8, cw)), r1[:-1]], axis=0)
            before2 = jnp.concatenate([jnp.where(sub == 0, c0, c1), r2[:-1]], axis=0)
            p1.append(jnp.where(sub == 0, before1, r1).reshape(tl, cw))
            p2.append(jnp.where(sub < 2, before2, r2).reshape(tl, cw))
            cs_ref[s, idx] = u[(s + 1) * tl - 2:(s + 1) * tl, :]
        u1 = p1[0] if ns == 1 else jnp.concatenate(p1, axis=0)
        u2 = p2[0] if ns == 1 else jnp.concatenate(p2, axis=0)
        wc = wc_ref[:, cols(idx)]
        return bc_ref[:, cols(idx)] + wc[0:1] * u2 + wc[1:2] * u1 + wc[2:3] * u

    ua, ug = up(0), up(nch)
    for c in range(nch):
        cur_a, cur_g = ua, ug
        if c + 1 < nch:
            ua, ug = up(c + 1), up(nch + c + 1)
        a = conv(c, cur_a)
        gt = conv(nch + c, cur_g)
        gelu = 0.5 * gt * (1.0 + jnp.tanh(GELU_C * (gt + 0.044715 * (gt * gt * gt))))
        act_scr[:, cols(c)] = (a * gelu).astype(BF16)
    y = jnp.dot(act_scr[...], wd_ref[...], preferred_element_type=F32)
    o_ref[...] = _layer_norm(alpha * h + y, g_ref[...], b_ref[...])


def _post_attention(attn, x, conv_prev, wo, g1, b1, wu, wc, bc, wd, g2, b2, *, ns, tl, alpha):
    r, d = x.shape
    n_streams = conv_prev.shape[0]
    cw = conv_prev.shape[3]
    nb = n_streams // ns
    nt = r // (n_streams * tl)
    assert ns == 1 or nt == 1
    assert tl % 16 == 0
    st_block = (ns,) + conv_prev.shape[1:]
    tile = pl.BlockSpec((ns * tl, d), lambda bi, ti: (bi * nt + ti, 0))
    state = pl.BlockSpec(st_block, lambda bi, ti: (bi, 0, 0, 0))
    weights = (wo, g1, b1, wu, wc, bc, wd, g2, b2)
    return pl.pallas_call(
        functools.partial(_ffn_kernel, ns=ns, tl=tl, cw=cw, alpha=alpha),
        grid=(nb, nt),
        in_specs=[tile, tile, state] + [_resident(w.shape) for w in weights],
        out_specs=[tile, state],
        out_shape=[jax.ShapeDtypeStruct((r, d), F32), jax.ShapeDtypeStruct(conv_prev.shape, F32)],
        scratch_shapes=[pltpu.VMEM((ns * tl, wd.shape[0]), BF16)],
        compiler_params=_cparams("parallel", "arbitrary"), name="post_attention",
    )(attn, x, conv_prev, *weights)


def _token_tile(r):
    for tm in (512, 256, 128, 64, 32, 16, 8):
        if r % tm == 0:
            return tm
    raise ValueError(f"row count {r} is not a multiple of 8")


def _pad_time(x, t_pad):
    return jnp.pad(x, ((0, 0), (0, t_pad - x.shape[1])) + ((0, 0),) * (x.ndim - 2))


def kernel(x_prompt, x_sample, cache_a_k, cache_a_v, cache_a_logf, cache_b_k, cache_b_v, state_conv,
           w_a_in, b_a_f, w_a_o, w_b_in, w_b_o, ln1_g, ln1_b, w_up, w_conv, b_conv, w_down, ln2_g, ln2_b):
    bp, tp, d = x_prompt.shape
    bs, ts, _ = x_sample.shape
    depth = ln1_g.shape[0]
    n_heads, hd = cache_a_k.shape[3], cache_a_k.shape[4]
    past = cache_a_k.shape[2]
    d_ff = w_down.shape[1]
    conv_w = w_conv.shape[1]
    assert 2 * hd == LANES and conv_w == 3 and d % LANES == 0
    alpha = (2.0 * depth) ** 0.25
    qscale = LOG2E * hd ** -0.5
    cw = 256
    assert d_ff % cw == 0
    nch = d_ff // cw
    tb_s = 3 * LANES
    tks_pad = -(-(past + ts) // tb_s) * tb_s
    fox_tile = 256 if tp % 256 == 0 else tp
    fox_groups = next(n for n in (4, 2, 1) if tp % (n * fox_tile) == 0)
    sb_rows = 128 if tp % 128 == 0 else tp
    sb_groups = 4 if tp % (4 * sb_rows) == 0 else 1

    xp = x_prompt.reshape(bp * tp, d)
    xs = x_sample.reshape(bs * ts, d)
    tm_p, tm_s = _token_tile(bp * tp), _token_tile(bs * ts)
    tl_p = _token_tile(tp)
    zero_conv = jnp.zeros((bp, 2 * nch, 2, cw), F32)

    def conv_state_in(s):
        return s.reshape(s.shape[0], 2, 2 * nch, cw).transpose(0, 2, 1, 3)

    def conv_state_out(s):
        return s.transpose(0, 2, 1, 3).reshape(s.shape[0], 2, 2 * d_ff)

    def heads(y, b, t):
        return y.reshape(b, t, n_heads, hd)

    outs = {n: [] for n in ("p_a_k", "p_a_v", "p_a_f", "p_b_k", "p_b_v", "p_conv",
                            "s_a_k", "s_a_v", "s_a_f", "s_b_k", "s_b_v", "s_conv")}
    for i in range(depth):
        j = i // 2
        fox = i % 2 == 0
        w3 = _to_bf16(w_a_in if fox else w_b_in, j)
        wf = bf = None
        if fox:
            wf = jnp.pad(w_a_in[j, :, 3 * d:], ((0, 0), (0, LANES - n_heads))).astype(BF16)
            bf = jnp.pad(b_a_f[j], (0, LANES - n_heads)).reshape(1, LANES)
        w_o = _to_bf16(w_a_o if fox else w_b_o, j)
        g1, b1 = ln1_g[i].reshape(1, d), ln1_b[i].reshape(1, d)
        g2, b2 = ln2_g[i].reshape(1, d), ln2_b[i].reshape(1, d)
        wu = _to_bf16(w_up, i)
        wc = w_conv[i]
        bc = b_conv[i].reshape(1, 2 * d_ff)
        wd = _to_bf16(w_down, i)

        res = _inproj(xp, w3, wf, bf, tm=tm_p, qscale=qscale, n_heads=n_heads)
        q, kb, vb, k, v = res[:5]
        q, kb, vb = (a.reshape(bp, tp, d) for a in (q, kb, vb))
        if fox:
            lf = res[5].reshape(bp, tp, n_heads)
            f2 = _cumsum_log2(lf.transpose(0, 2, 1).reshape(bp * n_heads, tp))
            f2 = f2.reshape(bp, n_heads // 2, 2, tp // fox_tile, fox_tile)
            o = _fox_attention(q, kb, vb, f2, ng=fox_groups, tq=fox_tile, tk=fox_tile, q_off=0, hd=hd)
            outs["p_a_k"].append(heads(k, bp, tp)); outs["p_a_v"].append(heads(v, bp, tp))
            outs["p_a_f"].append(lf)
        else:
            o = _sb_attention(q, kb, vb, ng=sb_groups, tr=sb_rows, tb=sb_rows, q_off=0, hd=hd, first=3)
            outs["p_b_k"].append(heads(k, bp, tp)); outs["p_b_v"].append(heads(v, bp, tp))
        xp, cs = _post_attention(o.reshape(bp * tp, d), xp, zero_conv, w_o, g1, b1, wu, wc, bc, wd, g2, b2,
                                 ns=1, tl=tl_p, alpha=alpha)
        outs["p_conv"].append(conv_state_out(cs))

        res = _inproj(xs, w3, wf, bf, tm=tm_s, qscale=qscale, n_heads=n_heads)
        q, kb, vb, k, v = res[:5]
        q = q.reshape(bs, ts, d)
        ck, cv = (cache_a_k[j], cache_a_v[j]) if fox else (cache_b_k[j], cache_b_v[j])
        k_all = _pad_time(jnp.concatenate([ck.reshape(bs, past, d).astype(BF16), kb.reshape(bs, ts, d)], 1), tks_pad)
        v_all = _pad_time(jnp.concatenate([cv.reshape(bs, past, d).astype(BF16), vb.reshape(bs, ts, d)], 1), tks_pad)
        if fox:
            lf = res[5].reshape(bs, ts, n_heads)
            lf_all = _pad_time(jnp.concatenate([cache_a_logf[j].astype(F32), lf], 1), tks_pad)
            f2 = _cumsum_log2(lf_all.transpose(0, 2, 1).reshape(bs * n_heads, tks_pad))
            f2 = f2.reshape(bs, n_heads // 2, 2, 1, tks_pad)
            o = _fox_attention(q, k_all, v_all, f2, ng=1, tq=ts, tk=tks_pad, q_off=past, hd=hd)
            outs["s_a_k"].append(heads(k, bs, ts)); outs["s_a_v"].append(heads(v, bs, ts))
            outs["s_a_f"].append(lf)
        else:
            o = _sb_attention(q, k_all, v_all, ng=1, tr=ts, tb=tb_s, q_off=past, hd=hd, first=1)
            outs["s_b_k"].append(heads(k, bs, ts)); outs["s_b_v"].append(heads(v, bs, ts))
        xs, cs = _post_attention(o.reshape(bs * ts, d), xs, conv_state_in(state_conv[i]), w_o, g1, b1,
                                 wu, wc, bc, wd, g2, b2, ns=bs, tl=ts, alpha=alpha)
        outs["s_conv"].append(conv_state_out(cs))

    st = {n: jnp.stack(v) for n, v in outs.items()}
    return (xp.reshape(bp, tp, d), xs.reshape(bs, ts, d),
            st["p_a_k"], st["p_a_v"], st["p_a_f"], st["p_b_k"], st["p_b_v"], st["p_conv"],
            st["s_a_k"], st["s_a_v"], st["s_a_f"], st["s_b_k"], st["s_b_v"], st["s_conv"])
```

```python
import functools

import numpy as np
import jax
import jax.numpy as jnp
from jax import lax
from jax.experimental import pallas as pl
from jax.experimental.pallas import tpu as pltpu

F32 = jnp.float32
BF16 = jnp.bfloat16
LOG2E = 1.4426950408889634
LN_EPS = 1e-5
LANES = 128
NEG_BIG = -1e30
ZERO_LOG2 = 150.0
NORM_SLACK = 1.0 + 2.0 ** -6
EXP_HEADROOM = 90.0
VMEM_LIMIT = 56 * 1024 * 1024
GELU_C = 0.7978845608028654


def _cparams(*sem):
    return pltpu.CompilerParams(dimension_semantics=sem, vmem_limit_bytes=VMEM_LIMIT)


def _resident(shape):
    nd = len(shape)
    return pl.BlockSpec(shape, lambda *_: (0,) * nd, pipeline_mode=pl.Buffered(1))


def _in_head(lane, h, hd):
    return lane < hd if h == 0 else lane >= hd


def _log_sigmoid(x):
    return jnp.minimum(x, 0.0) - jnp.log(1.0 + jnp.exp(-jnp.abs(x)))


def _layer_norm(y, g, b):
    mu = jnp.mean(y, axis=-1, keepdims=True)
    yc = y - mu
    var = jnp.mean(yc * yc, axis=-1, keepdims=True)
    return yc * lax.rsqrt(var + LN_EPS) * g + b


def _cast_kernel(x_ref, o_ref):
    o_ref[...] = x_ref[...].astype(o_ref.dtype)


def _to_bf16(x, layer):
    _, r, c = x.shape
    tr = 256 if r % 256 == 0 else r
    return pl.pallas_call(
        _cast_kernel, grid=(r // tr,),
        in_specs=[pl.BlockSpec((None, tr, c), lambda i: (layer, i, 0))],
        out_specs=pl.BlockSpec((tr, c), lambda i: (i, 0)),
        out_shape=jax.ShapeDtypeStruct((r, c), BF16),
        compiler_params=_cparams("parallel"), name="cast_bf16",
    )(x)


def _inproj_kernel(x_ref, w_ref, *rest, has_f, qscale, n_heads):
    if has_f:
        wf_ref, bf_ref, q_ref, kb_ref, vb_ref, k_ref, v_ref, lf_ref = rest
    else:
        q_ref, kb_ref, vb_ref, k_ref, v_ref = rest
    xb = x_ref[...].astype(BF16)
    tm, d = xb.shape
    hd = d // n_heads

    def split_heads(y, y_ref):
        for h in range(n_heads):
            y_ref[pl.ds(h, tm, stride=n_heads), :] = y[:, h * hd:(h + 1) * hd]

    k = jnp.dot(xb, w_ref[:, d:2 * d], preferred_element_type=F32)
    v = jnp.dot(xb, w_ref[:, 2 * d:3 * d], preferred_element_type=F32)
    kb_ref[...] = k.astype(BF16)
    split_heads(k, k_ref)
    q = jnp.dot(xb, w_ref[:, 0:d], preferred_element_type=F32)
    vb_ref[...] = v.astype(BF16)
    split_heads(v, v_ref)
    q_ref[...] = (q * qscale).astype(BF16)
    if has_f:
        z = jnp.dot(xb, wf_ref[...], preferred_element_type=F32) + bf_ref[...]
        lf_ref[...] = _log_sigmoid(z)[:, :n_heads]


def _inproj(x, w3, wf, bf, *, tm, qscale, n_heads):
    r, d = x.shape
    hd = d // n_heads
    has_f = wf is not None
    row = lambda i: (i, 0)
    in_specs = [pl.BlockSpec((tm, d), row), _resident(w3.shape)]
    args = [x, w3]
    out_shape = [jax.ShapeDtypeStruct((r, d), BF16)] * 3 + [jax.ShapeDtypeStruct((r * n_heads, hd), F32)] * 2
    out_specs = [pl.BlockSpec((tm, d), row)] * 3 + [pl.BlockSpec((tm * n_heads, hd), row)] * 2
    if has_f:
        in_specs += [_resident(wf.shape), _resident(bf.shape)]
        args += [wf, bf]
        out_shape.append(jax.ShapeDtypeStruct((r, n_heads), F32))
        out_specs.append(pl.BlockSpec((tm, n_heads), row))
    return pl.pallas_call(
        functools.partial(_inproj_kernel, has_f=has_f, qscale=qscale, n_heads=n_heads),
        grid=(r // tm,), in_specs=in_specs, out_specs=out_specs, out_shape=out_shape,
        compiler_params=_cparams("parallel"), name="inproj_f" if has_f else "inproj",
    )(*args)


def _cumsum_kernel(x_ref, u_ref, o_ref, carry_ref, *, tc):
    @pl.when(pl.program_id(0) == 0)
    def _():
        carry_ref[...] = jnp.zeros_like(carry_ref)

    x = x_ref[...]
    r = x.shape[0]
    hi = x.astype(BF16)
    r1 = x - hi.astype(F32)
    mid = r1.astype(BF16)
    lo = (r1 - mid.astype(F32)).astype(BF16)
    y3 = jnp.dot(jnp.concatenate([hi, mid, lo], axis=0), u_ref[...], preferred_element_type=F32)
    y = y3[:r] + y3[r:2 * r] + y3[2 * r:]
    c = carry_ref[...]
    o_ref[...] = (y[:, :tc] + jnp.tile(c, (1, tc // LANES))) * LOG2E
    carry_ref[...] = c + y[:, tc:]


def _prefix_matrix(tc):
    j = np.arange(tc)[:, None]
    s = np.arange(tc + LANES)[None, :]
    return jnp.asarray(np.where((s >= tc) | (j <= s), 1.0, 0.0), dtype=BF16)


def _cumsum_log2(x):
    r, t = x.shape
    tc = 512 if t % 512 == 0 else t
    return pl.pallas_call(
        functools.partial(_cumsum_kernel, tc=tc),
        grid=(t // tc,),
        in_specs=[pl.BlockSpec((r, tc), lambda i: (0, i)), _resident((tc, tc + LANES))],
        out_specs=pl.BlockSpec((r, tc), lambda i: (0, i)),
        out_shape=jax.ShapeDtypeStruct((r, t), F32),
        scratch_shapes=[pltpu.VMEM((r, LANES), F32)],
        compiler_params=_cparams("arbitrary"), name="cumsum",
    )(x, _prefix_matrix(tc))


def _head_sq(x, hd):
    xf = x.astype(F32)
    lrow = lax.broadcasted_iota(jnp.int32, (LANES, LANES), 0)
    lcol = lax.broadcasted_iota(jnp.int32, (LANES, LANES), 1)
    ind = jnp.where((lrow >= hd).astype(jnp.int32) == lcol, 1.0, 0.0).astype(BF16)
    return jnp.dot((xf * xf).astype(BF16), ind, preferred_element_type=F32)


def _fox_kernel(q_ref, k_ref, v_ref, f_ref, o_ref, m_scr, acc_scr, kn_scr, fm_scr,
                *, ng, tq, tk, nkb, q_off, hd):
    qi = pl.program_id(2)

    if nkb > 1:
        @pl.when(qi == 0)
        def _stats():
            cb = 8 if nkb % 8 == 0 else nkb
            lane_c = lax.broadcasted_iota(jnp.int32, (cb, LANES), 1)
            kn_scr[...] = jnp.zeros_like(kn_scr)
            fm_scr[...] = jnp.zeros_like(fm_scr)

            def body(c, _):
                j0 = pl.multiple_of(c * cb, cb)
                rows = pl.ds(pl.multiple_of(j0 * tk, cb * tk), cb * tk)
                n2 = _head_sq(k_ref[0, rows, :], hd)
                kn = jnp.sqrt(jnp.max(n2.reshape(cb, tk, LANES), axis=1) * NORM_SLACK)
                va = jnp.max(jnp.abs(v_ref[0, rows, :].astype(F32)).reshape(cb, tk, LANES), axis=1)
                kn_scr[pl.ds(j0, cb), :] = jnp.where(lane_c == 2, jnp.max(va, axis=1, keepdims=True), kn)
                fmin = [jnp.min(f_ref[0, 0, h, pl.ds(j0, cb), :], axis=1, keepdims=True) for h in range(2)]
                fm_scr[pl.ds(j0, cb), :] = jnp.where(lane_c == 0, fmin[0], fmin[1])
                return 0

            lax.fori_loop(0, nkb // cb, body, 0)

    lane = lax.broadcasted_iota(jnp.int32, (tq, LANES), 1)
    row = lax.broadcasted_iota(jnp.int32, (tq, tk), 0)
    col = lax.broadcasted_iota(jnp.int32, (tq, tk), 1)
    lane_k = lax.broadcasted_iota(jnp.int32, (tk, LANES), 1)
    ones_k = jnp.ones((tk, LANES), BF16)
    reps = tk // LANES
    chains = [(g, h) for g in range(ng) for h in range(2)]

    qg, qh, jd, dmask, cc = [], {}, [], [], {}
    for g in range(ng):
        q = q_ref[0, g * tq:(g + 1) * tq, :]
        q_start = q_off + (qi * ng + g) * tq
        qg.append(q)
        jd.append(q_start // tk)
        dmask.append((jd[g] * tk + col) <= (q_start + row))
        for h in range(2):
            qh[g, h] = jnp.where(_in_head(lane, h, hd), q, jnp.zeros_like(q))
            cc[g, h] = f_ref[0, 0, h, pl.ds(jd[g], 1), :][:, 0:1]

    m_scr[...] = jnp.full(m_scr.shape, NEG_BIG, F32)
    acc_scr[...] = jnp.zeros_like(acc_scr)

    def step(ds, mode):
        items = []
        for d in ds:
            for g in range(ng):
                j = jd[g] - d
                jc = jnp.maximum(j, 0)
                start = pl.multiple_of(jc * tk, tk)
                vblk = v_ref[0, pl.ds(start, tk), :]
                items.append((g, jc, jnp.where(j >= 0, 0.0, NEG_BIG),
                              k_ref[0, pl.ds(start, tk), :],
                              [jnp.where(_in_head(lane_k, h, hd), vblk, ones_k) for h in range(2)]))
        s = {}
        for i, (g, jc, pen, kblk, _) in enumerate(items):
            for h in range(2):
                x = lax.dot_general(qh[g, h], kblk, (((1,), (1,)), ((), ())), preferred_element_type=F32)
                x = x + (cc[g, h] + pen - f_ref[0, 0, h, pl.ds(jc, 1), :])
                s[i, h] = jnp.where(dmask[g], x, NEG_BIG) if mode == "diag" else x
        m_old = {c: m_scr[2 * c[0] + c[1]] for c in chains}
        if mode == "fixed_max":
            p = {c: jnp.exp2(s[c] - jnp.tile(m_old[items[c[0]][0], c[1]], (1, reps))).astype(BF16) for c in s}
            pv = {c: jnp.dot(p[c], items[c[0]][4][c[1]], preferred_element_type=F32) for c in s}
            for g, h in chains:
                acc_scr[2 * g + h] += sum(pv[i, h] for i in range(len(items)) if items[i][0] == g)
            return
        assert len(ds) == 1
        m_new = {c: jnp.maximum(m_old[c], jnp.max(s[c], axis=1, keepdims=True)) for c in chains}
        p = {c: jnp.exp2(s[c] - jnp.tile(m_new[c], (1, reps))).astype(BF16) for c in chains}
        a = {c: jnp.exp2(m_old[c] - m_new[c]) for c in chains}
        for g, h in chains:
            m_scr[2 * g + h] = m_new[g, h]
        pv = {c: jnp.dot(p[c], items[c[0]][4][c[1]], preferred_element_type=F32) for c in chains}
        for g, h in chains:
            acc_scr[2 * g + h] = a[g, h] * acc_scr[2 * g + h] + pv[g, h]

    step([0], "diag")

    if nkb > 1:
        jrow = lax.broadcasted_iota(jnp.int32, (kn_scr.shape[0], 1), 0)
        jrow_f = jrow.astype(F32)
        v_log2 = jnp.log2(jnp.maximum(jnp.max(kn_scr[:, 2:3], axis=0, keepdims=True), 1.0))
        depth_v = jnp.zeros(jrow.shape, F32)
        excess_v = jnp.full(jrow.shape, NEG_BIG, F32)
        for g in range(ng):
            jd_f = jd[g].astype(F32)
            qn = jnp.sqrt(jnp.max(_head_sq(qg[g], hd), axis=0, keepdims=True) * NORM_SLACK)
            for h in range(2):
                m_min = jnp.min(m_scr[2 * g + h], axis=0, keepdims=True)[:, 0:1]
                over = qn[:, h:h + 1] * kn_scr[:, h:h + 1] + cc[g, h] - fm_scr[:, h:h + 1] - m_min
                need = (over > -ZERO_LOG2) & (jrow < jd[g])
                depth_v = jnp.maximum(depth_v, jnp.where(need, jd_f - jrow_f, 0.0))
                excess_v = jnp.maximum(excess_v, jnp.where(need, over + v_log2, NEG_BIG))
        depth = jnp.max(depth_v).astype(jnp.int32)
        excess = jnp.max(excess_v)

        def fixed_sweep():
            def pair(i, _):
                step([2 * i + 1, 2 * i + 2], "fixed_max")
                return 0

            lax.fori_loop(0, depth // 2, pair, 0)

            @pl.when(depth % 2 == 1)
            def _():
                step([depth], "fixed_max")

        def running_sweep():
            def body(d, _):
                step([d], "running_max")
                return 0

            lax.fori_loop(1, depth + 1, body, 0)

        lax.cond(excess <= EXP_HEADROOM, fixed_sweep, running_sweep)

    first = _in_head(lane, 0, hd)
    for g in range(ng):
        num = jnp.where(first, acc_scr[2 * g], acc_scr[2 * g + 1])
        den = pltpu.roll(jnp.where(first, acc_scr[2 * g + 1], acc_scr[2 * g]), hd, 1)
        o_ref[0, g * tq:(g + 1) * tq, :] = (num / den).astype(BF16)


def _fox_attention(q, k, v, f2, *, ng, tq, tk, q_off, hd):
    b, t_q, d = q.shape
    t_k = k.shape[1]
    nkb = t_k // tk
    nkb_pad = -(-nkb // 8) * 8
    assert tk % tq == 0 and q_off % tq == 0
    assert EXP_HEADROOM + np.log2(t_k) < 126
    return pl.pallas_call(
        functools.partial(_fox_kernel, ng=ng, tq=tq, tk=tk, nkb=nkb, q_off=q_off, hd=hd),
        grid=(b, d // LANES, t_q // (ng * tq)),
        in_specs=[pl.BlockSpec((1, ng * tq, LANES), lambda bi, hp, qi: (bi, qi, hp)),
                  pl.BlockSpec((1, t_k, LANES), lambda bi, hp, qi: (bi, 0, hp)),
                  pl.BlockSpec((1, t_k, LANES), lambda bi, hp, qi: (bi, 0, hp)),
                  pl.BlockSpec((1, 1, 2, nkb, tk), lambda bi, hp, qi: (bi, hp, 0, 0, 0))],
        out_specs=pl.BlockSpec((1, ng * tq, LANES), lambda bi, hp, qi: (bi, qi, hp)),
        out_shape=jax.ShapeDtypeStruct((b, t_q, d), BF16),
        scratch_shapes=[pltpu.VMEM((2 * ng, tq, LANES), F32), pltpu.VMEM((2 * ng, tq, LANES), F32),
                        pltpu.VMEM((nkb_pad, LANES), F32), pltpu.VMEM((nkb_pad, LANES), F32)],
        compiler_params=_cparams("parallel", "parallel", "arbitrary"), name="fox_attention",
    )(q, k, v, f2)


def _sb_kernel(q_ref, k_ref, v_ref, u_ref, o_ref, cs_scr, acc_scr, *, ng, tr, tb, q_off, hd, first):
    qi = pl.program_id(2)
    lane = lax.broadcasted_iota(jnp.int32, (tr, LANES), 1)
    row = lax.broadcasted_iota(jnp.int32, (tr, tb), 0)
    col = lax.broadcasted_iota(jnp.int32, (tr, tb), 1)
    u2 = u_ref[...]
    chains = [(g, h) for g in range(ng) for h in range(2)]

    qh, jd, dmask = {}, [], []
    for g in range(ng):
        q = q_ref[0, g * tr:(g + 1) * tr, :]
        q_start = q_off + (qi * ng + g) * tr
        jd.append(q_start // tb)
        dmask.append((jd[g] * tb + col) < (q_start + row))
        for h in range(2):
            qh[g, h] = jnp.where(_in_head(lane, h, hd), q, jnp.zeros_like(q))

    cs_scr[...] = jnp.zeros_like(cs_scr)
    acc_scr[...] = jnp.zeros_like(acc_scr)

    def step(ds, diag_first):
        j, kblk, vblk, pen = {}, {}, {}, {}
        for k, d in enumerate(ds):
            for g in range(ng):
                j[k, g] = jd[g] - d
                start = pl.multiple_of(jnp.maximum(j[k, g], 0) * tb, tb)
                kblk[k, g] = k_ref[0, pl.ds(start, tb), :]
                vblk[k, g] = v_ref[0, pl.ds(start, tb), :]
                pen[k, g] = jnp.where(j[k, g] >= 0, 0.0, -NEG_BIG)
        links = [(k, g, h) for k in range(len(ds)) for g, h in chains]
        masked = {c: diag_first and c[0] == 0 for c in links}
        z = {c: lax.dot_general(qh[c[1], c[2]], kblk[c[0], c[1]], (((1,), (1,)), ((), ())),
                                preferred_element_type=F32) for c in links}
        sp = {c: jnp.maximum(z[c], 0.0) + jnp.log2(1.0 + jnp.exp2(-jnp.abs(z[c]))) for c in links}
        w = {}
        for c in links:
            spm = jnp.where(dmask[c[1]], sp[c], 0.0) if masked[c] else sp[c]
            hi = spm.astype(BF16)
            lo = (spm - hi.astype(F32)).astype(BF16)
            w[c] = jnp.dot(jnp.concatenate([hi, lo], axis=1), u2, preferred_element_type=F32)
        a = {}
        open_sum = None
        for i, (g, h) in enumerate(chains):
            cs = cs_scr[i]
            for k in range(len(ds)):
                c = (k, g, h)
                x = jnp.exp2(z[c] - sp[c] - w[c][:, :tb] - jnp.tile(cs + pen[k, g], (1, tb // LANES)))
                a[c] = (jnp.where(dmask[g], x, 0.0) if masked[c] else x).astype(BF16)
                cs = cs + w[c][:, tb:tb + LANES]
            cs_scr[i] = cs
            cand = cs + jnp.where(j[len(ds) - 1, g] >= 1, 0.0, -NEG_BIG)
            open_sum = cand if open_sum is None else jnp.minimum(open_sum, cand)
        for i, (g, h) in enumerate(chains):
            acc_scr[i] += sum(jnp.dot(a[k, g, h], vblk[k, g], preferred_element_type=F32)
                              for k in range(len(ds)))
        return (jnp.min(open_sum) < ZERO_LOG2).astype(jnp.int32)

    cont = step(list(range(first)), True)
    lax.while_loop(lambda c: c[1] > 0, lambda c: (c[0] + 1, step([c[0]], False)), (jnp.int32(first), cont))

    for g in range(ng):
        o = jnp.where(_in_head(lane, 0, hd), acc_scr[2 * g], acc_scr[2 * g + 1])
        o_ref[0, g * tr:(g + 1) * tr, :] = o.astype(BF16)


def _suffix_matrix(tb):
    j = (np.arange(2 * tb) % tb)[:, None]
    s = np.arange(tb + LANES)[None, :]
    return jnp.asarray(np.where((s >= tb) | (j > s), 1.0, 0.0), dtype=BF16)


def _sb_attention(q, k, v, *, ng, tr, tb, q_off, hd, first):
    b, t_q, d = q.shape
    t_k = k.shape[1]
    tq = ng * tr
    assert tb % tr == 0 and q_off % tr == 0
    return pl.pallas_call(
        functools.partial(_sb_kernel, ng=ng, tr=tr, tb=tb, q_off=q_off, hd=hd, first=first),
        grid=(b, d // LANES, t_q // tq),
        in_specs=[pl.BlockSpec((1, tq, LANES), lambda bi, hp, qi: (bi, qi, hp)),
                  pl.BlockSpec((1, t_k, LANES), lambda bi, hp, qi: (bi, 0, hp)),
                  pl.BlockSpec((1, t_k, LANES), lambda bi, hp, qi: (bi, 0, hp)),
                  _resident((2 * tb, tb + LANES))],
        out_specs=pl.BlockSpec((1, tq, LANES), lambda bi, hp, qi: (bi, qi, hp)),
        out_shape=jax.ShapeDtypeStruct((b, t_q, d), BF16),
        scratch_shapes=[pltpu.VMEM((2 * ng, tr, LANES), F32), pltpu.VMEM((2 * ng, tr, LANES), F32)],
        compiler_params=_cparams("parallel", "parallel", "arbitrary"), name="sb_attention",
    )(q, k, v, _suffix_matrix(tb))


def _ffn_kernel(a_ref, x_ref, cp_ref, wo_ref, g1_ref, b1_ref, wu_ref, wc_ref, bc_ref, wd_ref, g_ref, b_ref,
                o_ref, cs_ref, act_scr, *, ns, tl, cw, alpha):
    @pl.when(pl.program_id(1) == 0)
    def _():
        cs_ref[...] = cp_ref[...]

    h = _layer_norm(alpha * x_ref[...] + jnp.dot(a_ref[...], wo_ref[...], preferred_element_type=F32),
                    g1_ref[...], b1_ref[...])
    hb = h.astype(BF16)
    nch = wd_ref.shape[0] // cw

    def cols(idx):
        return slice(idx * cw, (idx + 1) * cw)

    def up(idx):
        return jnp.dot(hb, wu_ref[:, cols(idx)], preferred_element_type=F32)

    sub = lax.broadcasted_iota(jnp.int32, (1, 8, cw), 1)

    def conv(idx, u):
        p1, p2 = [], []
        for s in range(ns):
            us = u[s * tl:(s + 1) * tl].reshape(tl // 8, 8, cw)
            c0 = cs_ref[s, idx, 0:1, :].reshape(1, 1, cw)
            c1 = cs_ref[s, idx, 1:2, :].reshape(1, 1, cw)
            r1, r2 = pltpu.roll(us, 1, 1), pltpu.roll(us, 2, 1)
            before1 = jnp.concatenate([jnp.broadcast_to(c1, (1, 8, cw)), r1[:-1]], axis=0)
            before2 = jnp.concatenate([jnp.where(sub == 0, c0, c1), r2[:-1]], axis=0)
            p1.append(jnp.where(sub == 0, before1, r1).reshape(tl, cw))
            p2.append(jnp.where(sub < 2, before2, r2).reshape(tl, cw))
            cs_ref[s, idx] = u[(s + 1) * tl - 2:(s + 1) * tl, :]
        u1 = p1[0] if ns == 1 else jnp.concatenate(p1, axis=0)
        u2 = p2[0] if ns == 1 else jnp.concatenate(p2, axis=0)
        wc = wc_ref[:, cols(idx)]
        return bc_ref[:, cols(idx)] + wc[0:1] * u2 + wc[1:2] * u1 + wc[2:3] * u

    ua, ug = up(0), up(nch)
    for c in range(nch):
        cur_a, cur_g = ua, ug
        if c + 1 < nch:
            ua, ug = up(c + 1), up(nch + c + 1)
        a = conv(c, cur_a)
        gt = conv(nch + c, cur_g)
        gelu = 0.5 * gt * (1.0 + jnp.tanh(GELU_C * (gt + 0.044715 * (gt * gt * gt))))
        act_scr[:, cols(c)] = (a * gelu).astype(BF16)
    y = jnp.dot(act_scr[...], wd_ref[...], preferred_element_type=F32)
    o_ref[...] = _layer_norm(alpha * h + y, g_ref[...], b_ref[...])


def _post_attention(attn, x, conv_prev, wo, g1, b1, wu, wc, bc, wd, g2, b2, *, ns, tl, alpha):
    r, d = x.shape
    n_streams = conv_prev.shape[0]
    cw = conv_prev.shape[3]
    nb = n_streams // ns
    nt = r // (n_streams * tl)
    assert ns == 1 or nt == 1
    assert tl % 16 == 0
    st_block = (ns,) + conv_prev.shape[1:]
    tile = pl.BlockSpec((ns * tl, d), lambda bi, ti: (bi * nt + ti, 0))
    state = pl.BlockSpec(st_block, lambda bi, ti: (bi, 0, 0, 0))
    weights = (wo, g1, b1, wu, wc, bc, wd, g2, b2)
    return pl.pallas_call(
        functools.partial(_ffn_kernel, ns=ns, tl=tl, cw=cw, alpha=alpha),
        grid=(nb, nt),
        in_specs=[tile, tile, state] + [_resident(w.shape) for w in weights],
        out_specs=[tile, state],
        out_shape=[jax.ShapeDtypeStruct((r, d), F32), jax.ShapeDtypeStruct(conv_prev.shape, F32)],
        scratch_shapes=[pltpu.VMEM((ns * tl, wd.shape[0]), BF16)],
        compiler_params=_cparams("parallel", "arbitrary"), name="post_attention",
    )(attn, x, conv_prev, *weights)


def _token_tile(r):
    for tm in (512, 256, 128, 64, 32, 16, 8):
        if r % tm == 0:
            return tm
    raise ValueError(f"row count {r} is not a multiple of 8")


def _pad_time(x, t_pad):
    return jnp.pad(x, ((0, 0), (0, t_pad - x.shape[1])) + ((0, 0),) * (x.ndim - 2))


def kernel(x_prompt, x_sample, cache_a_k, cache_a_v, cache_a_logf, cache_b_k, cache_b_v, state_conv,
           w_a_in, b_a_f, w_a_o, w_b_in, w_b_o, ln1_g, ln1_b, w_up, w_conv, b_conv, w_down, ln2_g, ln2_b):
    bp, tp, d = x_prompt.shape
    bs, ts, _ = x_sample.shape
    depth = ln1_g.shape[0]
    n_heads, hd = cache_a_k.shape[3], cache_a_k.shape[4]
    past = cache_a_k.shape[2]
    d_ff = w_down.shape[1]
    conv_w = w_conv.shape[1]
    assert 2 * hd == LANES and conv_w == 3 and d % LANES == 0
    alpha = (2.0 * depth) ** 0.25
    qscale = LOG2E * hd ** -0.5
    cw = 256
    assert d_ff % cw == 0
    nch = d_ff // cw
    tb_s = 3 * LANES
    tks_pad = -(-(past + ts) // tb_s) * tb_s
    fox_tile = 256 if tp % 256 == 0 else tp
    fox_groups = next(n for n in (4, 2, 1) if tp % (n * fox_tile) == 0)
    sb_rows = 128 if tp % 128 == 0 else tp
    sb_groups = 4 if tp % (4 * sb_rows) == 0 else 1

    xp = x_prompt.reshape(bp * tp, d)
    xs = x_sample.reshape(bs * ts, d)
    tm_p, tm_s = _token_tile(bp * tp), _token_tile(bs * ts)
    tl_p = _token_tile(tp)
    zero_conv = jnp.zeros((bp, 2 * nch, 2, cw), F32)

    def conv_state_in(s):
        return s.reshape(s.shape[0], 2, 2 * nch, cw).transpose(0, 2, 1, 3)

    def conv_state_out(s):
        return s.transpose(0, 2, 1, 3).reshape(s.shape[0], 2, 2 * d_ff)

    def heads(y, b, t):
        return y.reshape(b, t, n_heads, hd)

    outs = {n: [] for n in ("p_a_k", "p_a_v", "p_a_f", "p_b_k", "p_b_v", "p_conv",
                            "s_a_k", "s_a_v", "s_a_f", "s_b_k", "s_b_v", "s_conv")}
    for i in range(depth):
        j = i // 2
        fox = i % 2 == 0
        w3 = _to_bf16(w_a_in if fox else w_b_in, j)
        wf = bf = None
        if fox:
            wf = jnp.pad(w_a_in[j, :, 3 * d:], ((0, 0), (0, LANES - n_heads))).astype(BF16)
            bf = jnp.pad(b_a_f[j], (0, LANES - n_heads)).reshape(1, LANES)
        w_o = _to_bf16(w_a_o if fox else w_b_o, j)
        g1, b1 = ln1_g[i].reshape(1, d), ln1_b[i].reshape(1, d)
        g2, b2 = ln2_g[i].reshape(1, d), ln2_b[i].reshape(1, d)
        wu = _to_bf16(w_up, i)
        wc = w_conv[i]
        bc = b_conv[i].reshape(1, 2 * d_ff)
        wd = _to_bf16(w_down, i)

        res = _inproj(xp, w3, wf, bf, tm=tm_p, qscale=qscale, n_heads=n_heads)
        q, kb, vb, k, v = res[:5]
        q, kb, vb = (a.reshape(bp, tp, d) for a in (q, kb, vb))
        if fox:
            lf = res[5].reshape(bp, tp, n_heads)
            f2 = _cumsum_log2(lf.transpose(0, 2, 1).reshape(bp * n_heads, tp))
            f2 = f2.reshape(bp, n_heads // 2, 2, tp // fox_tile, fox_tile)
            o = _fox_attention(q, kb, vb, f2, ng=fox_groups, tq=fox_tile, tk=fox_tile, q_off=0, hd=hd)
            outs["p_a_k"].append(heads(k, bp, tp)); outs["p_a_v"].append(heads(v, bp, tp))
            outs["p_a_f"].append(lf)
        else:
            o = _sb_attention(q, kb, vb, ng=sb_groups, tr=sb_rows, tb=sb_rows, q_off=0, hd=hd, first=3)
            outs["p_b_k"].append(heads(k, bp, tp)); outs["p_b_v"].append(heads(v, bp, tp))
        xp, cs = _post_attention(o.reshape(bp * tp, d), xp, zero_conv, w_o, g1, b1, wu, wc, bc, wd, g2, b2,
                                 ns=1, tl=tl_p, alpha=alpha)
        outs["p_conv"].append(conv_state_out(cs))

        res = _inproj(xs, w3, wf, bf, tm=tm_s, qscale=qscale, n_heads=n_heads)
        q, kb, vb, k, v = res[:5]
        q = q.reshape(bs, ts, d)
        ck, cv = (cache_a_k[j], cache_a_v[j]) if fox else (cache_b_k[j], cache_b_v[j])
        k_all = _pad_time(jnp.concatenate([ck.reshape(bs, past, d).astype(BF16), kb.reshape(bs, ts, d)], 1), tks_pad)
        v_all = _pad_time(jnp.concatenate([cv.reshape(bs, past, d).astype(BF16), vb.reshape(bs, ts, d)], 1), tks_pad)
        if fox:
            lf = res[5].reshape(bs, ts, n_heads)
            lf_all = _pad_time(jnp.concatenate([cache_a_logf[j].astype(F32), lf], 1), tks_pad)
            f2 = _cumsum_log2(lf_all.transpose(0, 2, 1).reshape(bs * n_heads, tks_pad))
            f2 = f2.reshape(bs, n_heads // 2, 2, 1, tks_pad)
            o = _fox_attention(q, k_all, v_all, f2, ng=1, tq=ts, tk=tks_pad, q_off=past, hd=hd)
            outs["s_a_k"].append(heads(k, bs, ts)); outs["s_a_v"].append(heads(v, bs, ts))
            outs["s_a_f"].append(lf)
        else:
            o = _sb_attention(q, k_all, v_all, ng=1, tr=ts, tb=tb_s, q_off=past, hd=hd, first=1)
            outs["s_b_k"].append(heads(k, bs, ts)); outs["s_b_v"].append(heads(v, bs, ts))
        xs, cs = _post_attention(o.reshape(bs * ts, d), xs, conv_state_in(state_conv[i]), w_o, g1, b1,
                                 wu, wc, bc, wd, g2, b2, ns=bs, tl=ts, alpha=alpha)
        outs["s_conv"].append(conv_state_out(cs))

    st = {n: jnp.stack(v) for n, v in outs.items()}
    return (xp.reshape(bp, tp, d), xs.reshape(bs, ts, d),
            st["p_a_k"], st["p_a_v"], st["p_a_f"], st["p_b_k"], st["p_b_v"], st["p_conv"],
            st["s_a_k"], st["s_a_v"], st["s_a_f"], st["s_b_k"], st["s_b_v"], st["s_conv"])
```

```python
import functools

import numpy as np
import jax
import jax.numpy as jnp
from jax import lax
from jax.experimental import pallas as pl
from jax.experimental.pallas import tpu as pltpu

F32 = jnp.float32
BF16 = jnp.bfloat16
LOG2E = 1.4426950408889634
LN_EPS = 1e-5
LANES = 128
NEG_BIG = -1e30
ZERO_LOG2 = 150.0
NORM_SLACK = 1.0 + 2.0 ** -6
EXP_HEADROOM = 90.0
VMEM_LIMIT = 56 * 1024 * 1024
GELU_C = 0.7978845608028654


def _cparams(*sem):
    return pltpu.CompilerParams(dimension_semantics=sem, vmem_limit_bytes=VMEM_LIMIT)


def _resident(shape):
    nd = len(shape)
    return pl.BlockSpec(shape, lambda *_: (0,) * nd, pipeline_mode=pl.Buffered(1))


def _in_head(lane, h, hd):
    return lane < hd if h == 0 else lane >= hd


def _log_sigmoid(x):
    return jnp.minimum(x, 0.0) - jnp.log(1.0 + jnp.exp(-jnp.abs(x)))


def _layer_norm(y, g, b):
    mu = jnp.mean(y, axis=-1, keepdims=True)
    yc = y - mu
    var = jnp.mean(yc * yc, axis=-1, keepdims=True)
    return yc * lax.rsqrt(var + LN_EPS) * g + b


def _cast_kernel(x_ref, o_ref):
    o_ref[...] = x_ref[...].astype(o_ref.dtype)


def _to_bf16(x, layer):
    _, r, c = x.shape
    tr = 256 if r % 256 == 0 else r
    return pl.pallas_call(
        _cast_kernel, grid=(r // tr,),
        in_specs=[pl.BlockSpec((None, tr, c), lambda i: (layer, i, 0))],
        out_specs=pl.BlockSpec((tr, c), lambda i: (i, 0)),
        out_shape=jax.ShapeDtypeStruct((r, c), BF16),
        compiler_params=_cparams("parallel"), name="cast_bf16",
    )(x)


def _inproj_kernel(x_ref, w_ref, *rest, has_f, qscale, n_heads):
    if has_f:
        wf_ref, bf_ref, q_ref, kb_ref, vb_ref, k_ref, v_ref, lf_ref = rest
    else:
        q_ref, kb_ref, vb_ref, k_ref, v_ref = rest
    xb = x_ref[...].astype(BF16)
    tm, d = xb.shape
    hd = d // n_heads

    def split_heads(y, y_ref):
        for h in range(n_heads):
            y_ref[pl.ds(h, tm, stride=n_heads), :] = y[:, h * hd:(h + 1) * hd]

    k = jnp.dot(xb, w_ref[:, d:2 * d], preferred_element_type=F32)
    v = jnp.dot(xb, w_ref[:, 2 * d:3 * d], preferred_element_type=F32)
    kb_ref[...] = k.astype(BF16)
    split_heads(k, k_ref)
    q = jnp.dot(xb, w_ref[:, 0:d], preferred_element_type=F32)
    vb_ref[...] = v.astype(BF16)
    split_heads(v, v_ref)
    q_ref[...] = (q * qscale).astype(BF16)
    if has_f:
        z = jnp.dot(xb, wf_ref[...], preferred_element_type=F32) + bf_ref[...]
        lf_ref[...] = _log_sigmoid(z)[:, :n_heads]


def _inproj(x, w3, wf, bf, *, tm, qscale, n_heads):
    r, d = x.shape
    hd = d // n_heads
    has_f = wf is not None
    row = lambda i: (i, 0)
    in_specs = [pl.BlockSpec((tm, d), row), _resident(w3.shape)]
    args = [x, w3]
    out_shape = [jax.ShapeDtypeStruct((r, d), BF16)] * 3 + [jax.ShapeDtypeStruct((r * n_heads, hd), F32)] * 2
    out_specs = [pl.BlockSpec((tm, d), row)] * 3 + [pl.BlockSpec((tm * n_heads, hd), row)] * 2
    if has_f:
        in_specs += [_resident(wf.shape), _resident(bf.shape)]
        args += [wf, bf]
        out_shape.append(jax.ShapeDtypeStruct((r, n_heads), F32))
        out_specs.append(pl.BlockSpec((tm, n_heads), row))
    return pl.pallas_call(
        functools.partial(_inproj_kernel, has_f=has_f, qscale=qscale, n_heads=n_heads),
        grid=(r // tm,), in_specs=in_specs, out_specs=out_specs, out_shape=out_shape,
        compiler_params=_cparams("parallel"), name="inproj_f" if has_f else "inproj",
    )(*args)


def _cumsum_kernel(x_ref, u_ref, o_ref, carry_ref, *, tc):
    @pl.when(pl.program_id(0) == 0)
    def _():
        carry_ref[...] = jnp.zeros_like(carry_ref)

    x = x_ref[...]
    r = x.shape[0]
    hi = x.astype(BF16)
    r1 = x - hi.astype(F32)
    mid = r1.astype(BF16)
    lo = (r1 - mid.astype(F32)).astype(BF16)
    y3 = jnp.dot(jnp.concatenate([hi, mid, lo], axis=0), u_ref[...], preferred_element_type=F32)
    y = y3[:r] + y3[r:2 * r] + y3[2 * r:]
    c = carry_ref[...]
    o_ref[...] = (y[:, :tc] + jnp.tile(c, (1, tc // LANES))) * LOG2E
    carry_ref[...] = c + y[:, tc:]


def _prefix_matrix(tc):
    j = np.arange(tc)[:, None]
    s = np.arange(tc + LANES)[None, :]
    return jnp.asarray(np.where((s >= tc) | (j <= s), 1.0, 0.0), dtype=BF16)


def _cumsum_log2(x):
    r, t = x.shape
    tc = 512 if t % 512 == 0 else t
    return pl.pallas_call(
        functools.partial(_cumsum_kernel, tc=tc),
        grid=(t // tc,),
        in_specs=[pl.BlockSpec((r, tc), lambda i: (0, i)), _resident((tc, tc + LANES))],
        out_specs=pl.BlockSpec((r, tc), lambda i: (0, i)),
        out_shape=jax.ShapeDtypeStruct((r, t), F32),
        scratch_shapes=[pltpu.VMEM((r, LANES), F32)],
        compiler_params=_cparams("arbitrary"), name="cumsum",
    )(x, _prefix_matrix(tc))


def _head_sq(x, hd):
    xf = x.astype(F32)
    lrow = lax.broadcasted_iota(jnp.int32, (LANES, LANES), 0)
    lcol = lax.broadcasted_iota(jnp.int32, (LANES, LANES), 1)
    ind = jnp.where((lrow >= hd).astype(jnp.int32) == lcol, 1.0, 0.0).astype(BF16)
    return jnp.dot((xf * xf).astype(BF16), ind, preferred_element_type=F32)


def _fox_kernel(q_ref, k_ref, v_ref, f_ref, o_ref, m_scr, acc_scr, kn_scr, fm_scr,
                *, ng, tq, tk, nkb, q_off, hd):
    qi = pl.program_id(2)

    if nkb > 1:
        @pl.when(qi == 0)
        def _stats():
            cb = 8 if nkb % 8 == 0 else nkb
            lane_c = lax.broadcasted_iota(jnp.int32, (cb, LANES), 1)
            kn_scr[...] = jnp.zeros_like(kn_scr)
            fm_scr[...] = jnp.zeros_like(fm_scr)

            def body(c, _):
                j0 = pl.multiple_of(c * cb, cb)
                rows = pl.ds(pl.multiple_of(j0 * tk, cb * tk), cb * tk)
                n2 = _head_sq(k_ref[0, rows, :], hd)
                kn = jnp.sqrt(jnp.max(n2.reshape(cb, tk, LANES), axis=1) * NORM_SLACK)
                va = jnp.max(jnp.abs(v_ref[0, rows, :].astype(F32)).reshape(cb, tk, LANES), axis=1)
                kn_scr[pl.ds(j0, cb), :] = jnp.where(lane_c == 2, jnp.max(va, axis=1, keepdims=True), kn)
                fmin = [jnp.min(f_ref[0, 0, h, pl.ds(j0, cb), :], axis=1, keepdims=True) for h in range(2)]
                fm_scr[pl.ds(j0, cb), :] = jnp.where(lane_c == 0, fmin[0], fmin[1])
                return 0

            lax.fori_loop(0, nkb // cb, body, 0)

    lane = lax.broadcasted_iota(jnp.int32, (tq, LANES), 1)
    row = lax.broadcasted_iota(jnp.int32, (tq, tk), 0)
    col = lax.broadcasted_iota(jnp.int32, (tq, tk), 1)
    lane_k = lax.broadcasted_iota(jnp.int32, (tk, LANES), 1)
    ones_k = jnp.ones((tk, LANES), BF16)
    reps = tk // LANES
    chains = [(g, h) for g in range(ng) for h in range(2)]

    qg, qh, jd, dmask, cc = [], {}, [], [], {}
    for g in range(ng):
        q = q_ref[0, g * tq:(g + 1) * tq, :]
        q_start = q_off + (qi * ng + g) * tq
        qg.append(q)
        jd.append(q_start // tk)
        dmask.append((jd[g] * tk + col) <= (q_start + row))
        for h in range(2):
            qh[g, h] = jnp.where(_in_head(lane, h, hd), q, jnp.zeros_like(q))
            cc[g, h] = f_ref[0, 0, h, pl.ds(jd[g], 1), :][:, 0:1]

    m_scr[...] = jnp.full(m_scr.shape, NEG_BIG, F32)
    acc_scr[...] = jnp.zeros_like(acc_scr)

    def step(ds, mode):
        items = []
        for d in ds:
            for g in range(ng):
                j = jd[g] - d
                jc = jnp.maximum(j, 0)
                start = pl.multiple_of(jc * tk, tk)
                vblk = v_ref[0, pl.ds(start, tk), :]
                items.append((g, jc, jnp.where(j >= 0, 0.0, NEG_BIG),
                              k_ref[0, pl.ds(start, tk), :],
                              [jnp.where(_in_head(lane_k, h, hd), vblk, ones_k) for h in range(2)]))
        s = {}
        for i, (g, jc, pen, kblk, _) in enumerate(items):
            for h in range(2):
                x = lax.dot_general(qh[g, h], kblk, (((1,), (1,)), ((), ())), preferred_element_type=F32)
                x = x + (cc[g, h] + pen - f_ref[0, 0, h, pl.ds(jc, 1), :])
                s[i, h] = jnp.where(dmask[g], x, NEG_BIG) if mode == "diag" else x
        m_old = {c: m_scr[2 * c[0] + c[1]] for c in chains}
        if mode == "fixed_max":
            p = {c: jnp.exp2(s[c] - jnp.tile(m_old[items[c[0]][0], c[1]], (1, reps))).astype(BF16) for c in s}
            pv = {c: jnp.dot(p[c], items[c[0]][4][c[1]], preferred_element_type=F32) for c in s}
            for g, h in chains:
                acc_scr[2 * g + h] += sum(pv[i, h] for i in range(len(items)) if items[i][0] == g)
            return
        assert len(ds) == 1
        m_new = {c: jnp.maximum(m_old[c], jnp.max(s[c], axis=1, keepdims=True)) for c in chains}
        p = {c: jnp.exp2(s[c] - jnp.tile(m_new[c], (1, reps))).astype(BF16) for c in chains}
        a = {c: jnp.exp2(m_old[c] - m_new[c]) for c in chains}
        for g, h in chains:
            m_scr[2 * g + h] = m_new[g, h]
        pv = {c: jnp.dot(p[c], items[c[0]][4][c[1]], preferred_element_type=F32) for c in chains}
        for g, h in chains:
            acc_scr[2 * g + h] = a[g, h] * acc_scr[2 * g + h] + pv[g, h]

    step([0], "diag")

    if nkb > 1:
        jrow = lax.broadcasted_iota(jnp.int32, (kn_scr.shape[0], 1), 0)
        jrow_f = jrow.astype(F32)
        v_log2 = jnp.log2(jnp.maximum(jnp.max(kn_scr[:, 2:3], axis=0, keepdims=True), 1.0))
        depth_v = jnp.zeros(jrow.shape, F32)
        excess_v = jnp.full(jrow.shape, NEG_BIG, F32)
        for g in range(ng):
            jd_f = jd[g].astype(F32)
            qn = jnp.sqrt(jnp.max(_head_sq(qg[g], hd), axis=0, keepdims=True) * NORM_SLACK)
            for h in range(2):
                m_min = jnp.min(m_scr[2 * g + h], axis=0, keepdims=True)[:, 0:1]
                over = qn[:, h:h + 1] * kn_scr[:, h:h + 1] + cc[g, h] - fm_scr[:, h:h + 1] - m_min
                need = (over > -ZERO_LOG2) & (jrow < jd[g])
                depth_v = jnp.maximum(depth_v, jnp.where(need, jd_f - jrow_f, 0.0))
                excess_v = jnp.maximum(excess_v, jnp.where(need, over + v_log2, NEG_BIG))
        depth = jnp.max(depth_v).astype(jnp.int32)
        excess = jnp.max(excess_v)

        def fixed_sweep():
            def pair(i, _):
                step([2 * i + 1, 2 * i + 2], "fixed_max")
                return 0

            lax.fori_loop(0, depth // 2, pair, 0)

            @pl.when(depth % 2 == 1)
            def _():
                step([depth], "fixed_max")

        def running_sweep():
            def body(d, _):
                step([d], "running_max")
                return 0

            lax.fori_loop(1, depth + 1, body, 0)

        lax.cond(excess <= EXP_HEADROOM, fixed_sweep, running_sweep)

    first = _in_head(lane, 0, hd)
    for g in range(ng):
        num = jnp.where(first, acc_scr[2 * g], acc_scr[2 * g + 1])
        den = pltpu.roll(jnp.where(first, acc_scr[2 * g + 1], acc_scr[2 * g]), hd, 1)
        o_ref[0, g * tq:(g + 1) * tq, :] = (num / den).astype(BF16)


def _fox_attention(q, k, v, f2, *, ng, tq, tk, q_off, hd):
    b, t_q, d = q.shape
    t_k = k.shape[1]
    nkb = t_k // tk
    nkb_pad = -(-nkb // 8) * 8
    assert tk % tq == 0 and q_off % tq == 0
    assert EXP_HEADROOM + np.log2(t_k) < 126
    return pl.pallas_call(
        functools.partial(_fox_kernel, ng=ng, tq=tq, tk=tk, nkb=nkb, q_off=q_off, hd=hd),
        grid=(b, d // LANES, t_q // (ng * tq)),
        in_specs=[pl.BlockSpec((1, ng * tq, LANES), lambda bi, hp, qi: (bi, qi, hp)),
                  pl.BlockSpec((1, t_k, LANES), lambda bi, hp, qi: (bi, 0, hp)),
                  pl.BlockSpec((1, t_k, LANES), lambda bi, hp, qi: (bi, 0, hp)),
                  pl.BlockSpec((1, 1, 2, nkb, tk), lambda bi, hp, qi: (bi, hp, 0, 0, 0))],
        out_specs=pl.BlockSpec((1, ng * tq, LANES), lambda bi, hp, qi: (bi, qi, hp)),
        out_shape=jax.ShapeDtypeStruct((b, t_q, d), BF16),
        scratch_shapes=[pltpu.VMEM((2 * ng, tq, LANES), F32), pltpu.VMEM((2 * ng, tq, LANES), F32),
                        pltpu.VMEM((nkb_pad, LANES), F32), pltpu.VMEM((nkb_pad, LANES), F32)],
        compiler_params=_cparams("parallel", "parallel", "arbitrary"), name="fox_attention",
    )(q, k, v, f2)


def _sb_kernel(q_ref, k_ref, v_ref, u_ref, o_ref, cs_scr, acc_scr, *, ng, tr, tb, q_off, hd, first):
    qi = pl.program_id(2)
    lane = lax.broadcasted_iota(jnp.int32, (tr, LANES), 1)
    row = lax.broadcasted_iota(jnp.int32, (tr, tb), 0)
    col = lax.broadcasted_iota(jnp.int32, (tr, tb), 1)
    u2 = u_ref[...]
    chains = [(g, h) for g in range(ng) for h in range(2)]

    qh, jd, dmask = {}, [], []
    for g in range(ng):
        q = q_ref[0, g * tr:(g + 1) * tr, :]
        q_start = q_off + (qi * ng + g) * tr
        jd.append(q_start // tb)
        dmask.append((jd[g] * tb + col) < (q_start + row))
        for h in range(2):
            qh[g, h] = jnp.where(_in_head(lane, h, hd), q, jnp.zeros_like(q))

    cs_scr[...] = jnp.zeros_like(cs_scr)
    acc_scr[...] = jnp.zeros_like(acc_scr)

    def step(ds, diag_first):
        j, kblk, vblk, pen = {}, {}, {}, {}
        for k, d in enumerate(ds):
            for g in range(ng):
                j[k, g] = jd[g] - d
                start = pl.multiple_of(jnp.maximum(j[k, g], 0) * tb, tb)
                kblk[k, g] = k_ref[0, pl.ds(start, tb), :]
                vblk[k, g] = v_ref[0, pl.ds(start, tb), :]
                pen[k, g] = jnp.where(j[k, g] >= 0, 0.0, -NEG_BIG)
        links = [(k, g, h) for k in range(len(ds)) for g, h in chains]
        masked = {c: diag_first and c[0] == 0 for c in links}
        z = {c: lax.dot_general(qh[c[1], c[2]], kblk[c[0], c[1]], (((1,), (1,)), ((), ())),
                                preferred_element_type=F32) for c in links}
        sp = {c: jnp.maximum(z[c], 0.0) + jnp.log2(1.0 + jnp.exp2(-jnp.abs(z[c]))) for c in links}
        w = {}
        for c in links:
            spm = jnp.where(dmask[c[1]], sp[c], 0.0) if masked[c] else sp[c]
            hi = spm.astype(BF16)
            lo = (spm - hi.astype(F32)).astype(BF16)
            w[c] = jnp.dot(jnp.concatenate([hi, lo], axis=1), u2, preferred_element_type=F32)
        a = {}
        open_sum = None
        for i, (g, h) in enumerate(chains):
            cs = cs_scr[i]
            for k in range(len(ds)):
                c = (k, g, h)
                x = jnp.exp2(z[c] - sp[c] - w[c][:, :tb] - jnp.tile(cs + pen[k, g], (1, tb // LANES)))
                a[c] = (jnp.where(dmask[g], x, 0.0) if masked[c] else x).astype(BF16)
                cs = cs + w[c][:, tb:tb + LANES]
            cs_scr[i] = cs
            cand = cs + jnp.where(j[len(ds) - 1, g] >= 1, 0.0, -NEG_BIG)
            open_sum = cand if open_sum is None else jnp.minimum(open_sum, cand)
        for i, (g, h) in enumerate(chains):
            acc_scr[i] += sum(jnp.dot(a[k, g, h], vblk[k, g], preferred_element_type=F32)
                              for k in range(len(ds)))
        return (jnp.min(open_sum) < ZERO_LOG2).astype(jnp.int32)

    cont = step(list(range(first)), True)
    lax.while_loop(lambda c: c[1] > 0, lambda c: (c[0] + 1, step([c[0]], False)), (jnp.int32(first), cont))

    for g in range(ng):
        o = jnp.where(_in_head(lane, 0, hd), acc_scr[2 * g], acc_scr[2 * g + 1])
        o_ref[0, g * tr:(g + 1) * tr, :] = o.astype(BF16)


def _suffix_matrix(tb):
    j = (np.arange(2 * tb) % tb)[:, None]
    s = np.arange(tb + LANES)[None, :]
    return jnp.asarray(np.where((s >= tb) | (j > s), 1.0, 0.0), dtype=BF16)


def _sb_attention(q, k, v, *, ng, tr, tb, q_off, hd, first):
    b, t_q, d = q.shape
    t_k = k.shape[1]
    tq = ng * tr
    assert tb % tr == 0 and q_off % tr == 0
    return pl.pallas_call(
        functools.partial(_sb_kernel, ng=ng, tr=tr, tb=tb, q_off=q_off, hd=hd, first=first),
        grid=(b, d // LANES, t_q // tq),
        in_specs=[pl.BlockSpec((1, tq, LANES), lambda bi, hp, qi: (bi, qi, hp)),
                  pl.BlockSpec((1, t_k, LANES), lambda bi, hp, qi: (bi, 0, hp)),
                  pl.BlockSpec((1, t_k, LANES), lambda bi, hp, qi: (bi, 0, hp)),
                  _resident((2 * tb, tb + LANES))],
        out_specs=pl.BlockSpec((1, tq, LANES), lambda bi, hp, qi: (bi, qi, hp)),
        out_shape=jax.ShapeDtypeStruct((b, t_q, d), BF16),
        scratch_shapes=[pltpu.VMEM((2 * ng, tr, LANES), F32), pltpu.VMEM((2 * ng, tr, LANES), F32)],
        compiler_params=_cparams("parallel", "parallel", "arbitrary"), name="sb_attention",
    )(q, k, v, _suffix_matrix(tb))


def _ffn_kernel(a_ref, x_ref, cp_ref, wo_ref, g1_ref, b1_ref, wu_ref, wc_ref, bc_ref, wd_ref, g_ref, b_ref,
                o_ref, cs_ref, act_scr, *, ns, tl, cw, alpha):
    @pl.when(pl.program_id(1) == 0)
    def _():
        cs_ref[...] = cp_ref[...]

    h = _layer_norm(alpha * x_ref[...] + jnp.dot(a_ref[...], wo_ref[...], preferred_element_type=F32),
                    g1_ref[...], b1_ref[...])
    hb = h.astype(BF16)
    nch = wd_ref.shape[0] // cw

    def cols(idx):
        return slice(idx * cw, (idx + 1) * cw)

    def up(idx):
        return jnp.dot(hb, wu_ref[:, cols(idx)], preferred_element_type=F32)

    sub = lax.broadcasted_iota(jnp.int32, (1, 8, cw), 1)

    def conv(idx, u):
        p1, p2 = [], []
        for s in range(ns):
            us = u[s * tl:(s + 1) * tl].reshape(tl // 8, 8, cw)
            c0 = cs_ref[s, idx, 0:1, :].reshape(1, 1, cw)
            c1 = cs_ref[s, idx, 1:2, :].reshape(1, 1, cw)
            r1, r2 = pltpu.roll(us, 1, 1), pltpu.roll(us, 2, 1)
            before1 = jnp.concatenate([jnp.broadcast_to(c1, (1, 8, cw)), r1[:-1]], axis=0)
            before2 = jnp.concatenate([jnp.where(sub == 0, c0, c1), r2[:-1]], axis=0)
            p1.append(jnp.where(sub == 0, before1, r1).reshape(tl, cw))
            p2.append(jnp.where(sub < 2, before2, r2).reshape(tl, cw))
            cs_ref[s, idx] = u[(s + 1) * tl - 2:(s + 1) * tl, :]
        u1 = p1[0] if ns == 1 else jnp.concatenate(p1, axis=0)
        u2 = p2[0] if ns == 1 else jnp.concatenate(p2, axis=0)
        wc = wc_ref[:, cols(idx)]
        return bc_ref[:, cols(idx)] + wc[0:1] * u2 + wc[1:2] * u1 + wc[2:3] * u

    ua, ug = up(0), up(nch)
    for c in range(nch):
        cur_a, cur_g = ua, ug
        if c + 1 < nch:
            ua, ug = up(c + 1), up(nch + c + 1)
        a = conv(c, cur_a)
        gt = conv(nch + c, cur_g)
        half_g = 0.5 * gt
        th = jnp.tanh(gt * (GELU_C + (GELU_C * 0.044715) * (gt * gt)))
        act_scr[:, cols(c)] = (a * (half_g + half_g * th)).astype(BF16)
    y = jnp.dot(act_scr[...], wd_ref[...], preferred_element_type=F32)
    o_ref[...] = _layer_norm(alpha * h + y, g_ref[...], b_ref[...])


def _post_attention(attn, x, conv_prev, wo, g1, b1, wu, wc, bc, wd, g2, b2, *, ns, tl, alpha):
    r, d = x.shape
    n_streams = conv_prev.shape[0]
    cw = conv_prev.shape[3]
    nb = n_streams // ns
    nt = r // (n_streams * tl)
    assert ns == 1 or nt == 1
    assert tl % 16 == 0
    st_block = (ns,) + conv_prev.shape[1:]
    tile = pl.BlockSpec((ns * tl, d), lambda bi, ti: (bi * nt + ti, 0))
    state = pl.BlockSpec(st_block, lambda bi, ti: (bi, 0, 0, 0))
    weights = (wo, g1, b1, wu, wc, bc, wd, g2, b2)
    return pl.pallas_call(
        functools.partial(_ffn_kernel, ns=ns, tl=tl, cw=cw, alpha=alpha),
        grid=(nb, nt),
        in_specs=[tile, tile, state] + [_resident(w.shape) for w in weights],
        out_specs=[tile, state],
        out_shape=[jax.ShapeDtypeStruct((r, d), F32), jax.ShapeDtypeStruct(conv_prev.shape, F32)],
        scratch_shapes=[pltpu.VMEM((ns * tl, wd.shape[0]), BF16)],
        compiler_params=_cparams("parallel", "arbitrary"), name="post_attention",
    )(attn, x, conv_prev, *weights)


def _token_tile(r):
    for tm in (512, 256, 128, 64, 32, 16, 8):
        if r % tm == 0:
            return tm
    raise ValueError(f"row count {r} is not a multiple of 8")


def _pad_time(x, t_pad):
    return jnp.pad(x, ((0, 0), (0, t_pad - x.shape[1])) + ((0, 0),) * (x.ndim - 2))


def kernel(x_prompt, x_sample, cache_a_k, cache_a_v, cache_a_logf, cache_b_k, cache_b_v, state_conv,
           w_a_in, b_a_f, w_a_o, w_b_in, w_b_o, ln1_g, ln1_b, w_up, w_conv, b_conv, w_down, ln2_g, ln2_b):
    bp, tp, d = x_prompt.shape
    bs, ts, _ = x_sample.shape
    depth = ln1_g.shape[0]
    n_heads, hd = cache_a_k.shape[3], cache_a_k.shape[4]
    past = cache_a_k.shape[2]
    d_ff = w_down.shape[1]
    conv_w = w_conv.shape[1]
    assert 2 * hd == LANES and conv_w == 3 and d % LANES == 0
    alpha = (2.0 * depth) ** 0.25
    qscale = LOG2E * hd ** -0.5
    cw = 256
    assert d_ff % cw == 0
    nch = d_ff // cw
    tb_s = 3 * LANES
    tks_pad = -(-(past + ts) // tb_s) * tb_s
    fox_tile = 256 if tp % 256 == 0 else tp
    fox_groups = next(n for n in (4, 2, 1) if tp % (n * fox_tile) == 0)
    sb_rows = 128 if tp % 128 == 0 else tp
    sb_groups = 4 if tp % (4 * sb_rows) == 0 else 1

    xp = x_prompt.reshape(bp * tp, d)
    xs = x_sample.reshape(bs * ts, d)
    tm_p, tm_s = _token_tile(bp * tp), _token_tile(bs * ts)
    tl_p = _token_tile(tp)
    zero_conv = jnp.zeros((bp, 2 * nch, 2, cw), F32)

    def conv_state_in(s):
        return s.reshape(s.shape[0], 2, 2 * nch, cw).transpose(0, 2, 1, 3)

    def conv_state_out(s):
        return s.transpose(0, 2, 1, 3).reshape(s.shape[0], 2, 2 * d_ff)

    def heads(y, b, t):
        return y.reshape(b, t, n_heads, hd)

    outs = {n: [] for n in ("p_a_k", "p_a_v", "p_a_f", "p_b_k", "p_b_v", "p_conv",
                            "s_a_k", "s_a_v", "s_a_f", "s_b_k", "s_b_v", "s_conv")}
    for i in range(depth):
        j = i // 2
        fox = i % 2 == 0
        w3 = _to_bf16(w_a_in if fox else w_b_in, j)
        wf = bf = None
        if fox:
            wf = jnp.pad(w_a_in[j, :, 3 * d:], ((0, 0), (0, LANES - n_heads))).astype(BF16)
            bf = jnp.pad(b_a_f[j], (0, LANES - n_heads)).reshape(1, LANES)
        w_o = _to_bf16(w_a_o if fox else w_b_o, j)
        g1, b1 = ln1_g[i].reshape(1, d), ln1_b[i].reshape(1, d)
        g2, b2 = ln2_g[i].reshape(1, d), ln2_b[i].reshape(1, d)
        wu = _to_bf16(w_up, i)
        wc = w_conv[i]
        bc = b_conv[i].reshape(1, 2 * d_ff)
        wd = _to_bf16(w_down, i)

        res = _inproj(xp, w3, wf, bf, tm=tm_p, qscale=qscale, n_heads=n_heads)
        q, kb, vb, k, v = res[:5]
        q, kb, vb = (a.reshape(bp, tp, d) for a in (q, kb, vb))
        if fox:
            lf = res[5].reshape(bp, tp, n_heads)
            f2 = _cumsum_log2(lf.transpose(0, 2, 1).reshape(bp * n_heads, tp))
            f2 = f2.reshape(bp, n_heads // 2, 2, tp // fox_tile, fox_tile)
            o = _fox_attention(q, kb, vb, f2, ng=fox_groups, tq=fox_tile, tk=fox_tile, q_off=0, hd=hd)
            outs["p_a_k"].append(heads(k, bp, tp)); outs["p_a_v"].append(heads(v, bp, tp))
            outs["p_a_f"].append(lf)
        else:
            o = _sb_attention(q, kb, vb, ng=sb_groups, tr=sb_rows, tb=sb_rows, q_off=0, hd=hd, first=3)
            outs["p_b_k"].append(heads(k, bp, tp)); outs["p_b_v"].append(heads(v, bp, tp))
        xp, cs = _post_attention(o.reshape(bp * tp, d), xp, zero_conv, w_o, g1, b1, wu, wc, bc, wd, g2, b2,
                                 ns=1, tl=tl_p, alpha=alpha)
        outs["p_conv"].append(conv_state_out(cs))

        res = _inproj(xs, w3, wf, bf, tm=tm_s, qscale=qscale, n_heads=n_heads)
        q, kb, vb, k, v = res[:5]
        q = q.reshape(bs, ts, d)
        ck, cv = (cache_a_k[j], cache_a_v[j]) if fox else (cache_b_k[j], cache_b_v[j])
        k_all = _pad_time(jnp.concatenate([ck.reshape(bs, past, d).astype(BF16), kb.reshape(bs, ts, d)], 1), tks_pad)
        v_all = _pad_time(jnp.concatenate([cv.reshape(bs, past, d).astype(BF16), vb.reshape(bs, ts, d)], 1), tks_pad)
        if fox:
            lf = res[5].reshape(bs, ts, n_heads)
            lf_all = _pad_time(jnp.concatenate([cache_a_logf[j].astype(F32), lf], 1), tks_pad)
            f2 = _cumsum_log2(lf_all.transpose(0, 2, 1).reshape(bs * n_heads, tks_pad))
            f2 = f2.reshape(bs, n_heads // 2, 2, 1, tks_pad)
            o = _fox_attention(q, k_all, v_all, f2, ng=1, tq=ts, tk=tks_pad, q_off=past, hd=hd)
            outs["s_a_k"].append(heads(k, bs, ts)); outs["s_a_v"].append(heads(v, bs, ts))
            outs["s_a_f"].append(lf)
        else:
            o = _sb_attention(q, k_all, v_all, ng=1, tr=ts, tb=tb_s, q_off=past, hd=hd, first=1)
            outs["s_b_k"].append(heads(k, bs, ts)); outs["s_b_v"].append(heads(v, bs, ts))
        xs, cs = _post_attention(o.reshape(bs * ts, d), xs, conv_state_in(state_conv[i]), w_o, g1, b1,
                                 wu, wc, bc, wd, g2, b2, ns=bs, tl=ts, alpha=alpha)
        outs["s_conv"].append(conv_state_out(cs))

    st = {n: jnp.stack(v) for n, v in outs.items()}
    return (xp.reshape(bp, tp, d), xs.reshape(bs, ts, d),
            st["p_a_k"], st["p_a_v"], st["p_a_f"], st["p_b_k"], st["p_b_v"], st["p_conv"],
            st["s_a_k"], st["s_a_v"], st["s_a_f"], st["s_b_k"], st["s_b_v"], st["s_conv"])
```
